```python
import math
import jax, jax.numpy as jnp
from jax import lax
import numpy as np

D_MODEL = 2048
BATCH = 32
SEQ = 256
DEPTH = 2
DEC_BATCH = 4
DEC_SEQ = 4096
PAST_LEN = 256

GRID_W = 64
N_MIXERS = 2
N_CONV = (DEPTH + 1) // 2
N_RET = DEPTH // 2
CONV_W = 31
RET_HEADS = 8
RET_DK = D_MODEL // RET_HEADS
RET_DV = 2 * D_MODEL // RET_HEADS
RET_CHUNK = 128
ROPE_BASE = 10000.0
PEER_HEADS = 8
PEER_DK = 256
N_KEYS = 128
N_EXPERTS = N_KEYS * N_KEYS
PEER_TOPK = 16
PEER_BLOCK = 128
EPS = 1e-6

kernel_name = "conformer_retnet_peer_diffusion_step"


def rmsnorm(x, g):
    xf = x.astype(jnp.float32)
    y = xf * lax.rsqrt(jnp.mean(xf * xf, axis=-1, keepdims=True) + EPS)
    return (y * g.astype(jnp.float32)).astype(x.dtype)


def ada_params(cond, w_mod, b_mod):
    m = jax.nn.silu(cond) @ w_mod + b_mod
    return [t[:, None, :] for t in jnp.split(m, 6, axis=-1)]


def modulate(x, g, shift, scale):
    return rmsnorm(x, g) * (1.0 + scale) + shift


def conv_module(h, w_in, b_in, dw, dw_b, norm_g, w_out, b_out, grid):
    B, L, D = h.shape
    a = h @ w_in + b_in
    a_val, a_gate = jnp.split(a, 2, axis=-1)
    a = a_val * jax.nn.sigmoid(a_gate)
    if grid:
        rows = L // GRID_W
        a = a.reshape(B * rows, GRID_W, D)
    pad = CONV_W // 2
    a = lax.conv_general_dilated(a, dw[:, None, :].astype(a.dtype), window_strides=(1,),
                                 padding=((pad, pad),), dimension_numbers=('NWC', 'WIO', 'NWC'),
                                 feature_group_count=D) + dw_b
    a = a.reshape(B, L, D)
    a = jax.nn.silu(rmsnorm(a, norm_g))
    return a @ w_out + b_out


def rope_2d(x):
    L = x.shape[1]
    rows = L // GRID_W
    row = jnp.repeat(jnp.arange(rows), GRID_W).astype(jnp.float32)
    col = jnp.tile(jnp.arange(GRID_W), rows).astype(jnp.float32)
    half = x.shape[-1] // 2
    nf = half // 2
    freqs = ROPE_BASE ** (-jnp.arange(nf, dtype=jnp.float32) / nf)

    def rot(xh, pos):
        ang = pos[:, None] * freqs[None, :]
        cos = jnp.cos(ang)[None, :, None, :]
        sin = jnp.sin(ang)[None, :, None, :]
        a, b = xh[..., :nf], xh[..., nf:]
        return jnp.concatenate([a * cos - b * sin, a * sin + b * cos], axis=-1)

    return jnp.concatenate([rot(x[..., :half], row), rot(x[..., half:], col)], axis=-1)


def retention_chunkwise(q, k, v, log_gamma, s0):
    B, L, H, DK = q.shape
    DV = v.shape[-1]
    n = L // RET_CHUNK
    C = RET_CHUNK

    def chunks(t):
        return t.reshape(B, n, C, H, t.shape[-1]).transpose(1, 0, 2, 3, 4)

    pos = jnp.arange(C, dtype=jnp.float32)
    diff = pos[:, None] - pos[None, :]
    lg = log_gamma[:, None, None]
    decay_mask = jnp.where(diff[None] >= 0, jnp.exp(lg * jnp.maximum(diff, 0.0)[None]), 0.0)
    xi = jnp.exp(log_gamma[:, None] * (pos[None, :] + 1.0)).T[None, :, :, None]
    zeta = jnp.exp(log_gamma[:, None] * (C - 1.0 - pos[None, :])).T[None, :, :, None]
    chunk_decay = jnp.exp(log_gamma * C)[None, :, None, None]

    def step(S, inp):
        qc, kc, vc = inp
        scores = jnp.einsum('bihd,bjhd->bhij', qc, kc) * decay_mask[None]
        inner = jnp.einsum('bhij,bjhv->bihv', scores, vc)
        cross = jnp.einsum('bihd,bhdv->bihv', qc * xi, S)
        S_new = chunk_decay * S + jnp.einsum('bjhd,bjhv->bhdv', kc * zeta, vc)
        return S_new, inner + cross

    S_fin, out = lax.scan(step, s0.astype(jnp.float32), (chunks(q), chunks(k), chunks(v)))
    out = out.transpose(1, 0, 2, 3, 4).reshape(B, L, H, DV)
    return out, S_fin


def retention_layer(h, w_in, decay, gn_g, w_out, s0, rope):
    B, L, D = h.shape
    proj = h @ w_in
    q, k, v, g = jnp.split(proj, [D, 2 * D, 4 * D], axis=-1)
    q = q.reshape(B, L, RET_HEADS, RET_DK).astype(jnp.float32)
    k = k.reshape(B, L, RET_HEADS, RET_DK).astype(jnp.float32) * (RET_DK ** -0.5)
    v = v.reshape(B, L, RET_HEADS, RET_DV).astype(jnp.float32)
    if rope:
        q = rope_2d(q)
        k = rope_2d(k)
    log_gamma = jax.nn.log_sigmoid(decay.astype(jnp.float32))
    o_f, s_f = retention_chunkwise(q, k, v, log_gamma[0], s0[:, 0])
    o_b, s_b = retention_chunkwise(q[:, ::-1], k[:, ::-1], v[:, ::-1], log_gamma[1], s0[:, 1])
    o = o_f + o_b[:, ::-1]
    mu = jnp.mean(o, axis=-1, keepdims=True)
    var = jnp.mean(jnp.square(o - mu), axis=-1, keepdims=True)
    o = (o - mu) * lax.rsqrt(var + EPS) * gn_g.astype(jnp.float32).reshape(RET_HEADS, RET_DV)
    gate = jax.nn.silu(g.astype(jnp.float32)).reshape(B, L, RET_HEADS, RET_DV)
    y = (gate * o).reshape(B, L, 2 * D).astype(h.dtype) @ w_out
    return y, jnp.stack([s_f, s_b], axis=1).astype(h.dtype)


def peer(h, w_q, keys, u, v):
    B, L, D = h.shape
    T = B * L
    xb = h.reshape(T // PEER_BLOCK, PEER_BLOCK, D)
    half = PEER_DK // 2
    K = PEER_TOPK

    def block(xt):
        q = (xt @ w_q).reshape(PEER_BLOCK, PEER_HEADS, PEER_DK).astype(jnp.float32)
        s1 = jnp.einsum('thd,nd->thn', q[..., :half], keys[0].astype(jnp.float32))
        s2 = jnp.einsum('thd,nd->thn', q[..., half:], keys[1].astype(jnp.float32))
        v1, i1 = lax.top_k(s1, K)
        v2, i2 = lax.top_k(s2, K)
        cand = (v1[..., :, None] + v2[..., None, :]).reshape(PEER_BLOCK, PEER_HEADS, K * K)
        vals, ai = lax.top_k(cand, K)
        idx = (jnp.take_along_axis(i1, ai // K, axis=-1) * N_KEYS
               + jnp.take_along_axis(i2, ai % K, axis=-1))
        gates = jax.nn.softmax(vals, axis=-1)
        u_sel = u[idx]
        v_sel = v[idx]
        act = jax.nn.gelu(jnp.einsum('td,thkd->thk', xt, u_sel))
        return jnp.einsum('thk,thkd->td', (gates * act).astype(xt.dtype), v_sel)

    return lax.map(block, xb).reshape(B, L, D)


def setup_inputs(seed: int = 0) -> dict:
    key = jax.random.key(seed)
    ks = jax.random.split(key, 32)
    D = D_MODEL
    f32 = jnp.float32

    def nrm(k, shape, scale):
        return jax.random.normal(k, shape, f32) * scale

    a = 5.0 + np.arange(RET_HEADS, dtype=np.float32)
    base_logit = jnp.asarray(np.log(np.power(2.0, a) - 1.0).astype(np.float32))
    return {
        'x_prompt': nrm(ks[0], (BATCH, SEQ, D), 1.0),
        'x_sample': nrm(ks[1], (DEC_BATCH, DEC_SEQ, D), 1.0),
        'state_ret': nrm(ks[2], (DEC_BATCH, N_RET, 2, RET_HEADS, RET_DK, RET_DV), 0.5),
        'c': nrm(ks[3], (DEC_BATCH, D), 1.0),
        'c_ctx': nrm(ks[4], (D,), 1.0),
        'w_mod': nrm(ks[5], (DEPTH, D, 6 * D), 0.5 * D ** -0.5),
        'b_mod': nrm(ks[6], (DEPTH, 6 * D), 0.01),
        'norm_mix_g': 1.0 + nrm(ks[7], (DEPTH, D), 0.01),
        'norm_ffn_g': 1.0 + nrm(ks[8], (DEPTH, D), 0.01),
        'conv_w_in': nrm(ks[9], (N_CONV, D, 2 * D), D ** -0.5),
        'conv_b_in': nrm(ks[10], (N_CONV, 2 * D), 0.01),
        'conv_dw': nrm(ks[11], (N_CONV, CONV_W, D), CONV_W ** -0.5),
        'conv_dw_b': nrm(ks[12], (N_CONV, D), 0.01),
        'conv_norm_g': 1.0 + nrm(ks[13], (N_CONV, D), 0.01),
        'conv_w_out': nrm(ks[14], (N_CONV, D, D), D ** -0.5),
        'conv_b_out': nrm(ks[15], (N_CONV, D), 0.01),
        'ret_w_in': nrm(ks[16], (N_RET, D, 6 * D), D ** -0.5),
        'ret_decay': base_logit[None, None, :] + nrm(ks[17], (N_RET, 2, RET_HEADS), 0.01),
        'ret_gn_g': 1.0 + nrm(ks[18], (N_RET, 2 * D), 0.01),
        'ret_w_out': nrm(ks[19], (N_RET, 2 * D, D), (2 * D) ** -0.5),
        'peer_w_q': nrm(ks[20], (DEPTH, D, PEER_HEADS * PEER_DK), D ** -0.5),
        'peer_keys': nrm(ks[21], (DEPTH, 2, N_KEYS, PEER_DK // 2), (PEER_DK // 2) ** -0.5),
        'peer_u': nrm(ks[22], (DEPTH, N_EXPERTS, D), D ** -0.5),
        'peer_v': nrm(ks[23], (DEPTH, N_EXPERTS, D), PEER_HEADS ** -0.5),
        'final_norm_g': 1.0 + nrm(ks[24], (D,), 0.01),
    }


def reference(x_prompt, x_sample, state_ret, c, c_ctx, w_mod, b_mod, norm_mix_g, norm_ffn_g,
              conv_w_in, conv_b_in, conv_dw, conv_dw_b, conv_norm_g, conv_w_out, conv_b_out,
              ret_w_in, ret_decay, ret_gn_g, ret_w_out, peer_w_q, peer_keys, peer_u, peer_v,
              final_norm_g):
    xp = x_prompt
    xs = x_sample
    new_states = []
    for l in range(DEPTH):
        sh1p, sc1p, g1p, sh2p, sc2p, g2p = ada_params(c_ctx[None, :], w_mod[l], b_mod[l])
        sh1s, sc1s, g1s, sh2s, sc2s, g2s = ada_params(c, w_mod[l], b_mod[l])
        hp = modulate(xp, norm_mix_g[l], sh1p, sc1p)
        hs = modulate(xs, norm_mix_g[l], sh1s, sc1s)
        i = l // N_MIXERS
        if l % N_MIXERS == 0:
            cp = (conv_w_in[i], conv_b_in[i], conv_dw[i], conv_dw_b[i], conv_norm_g[i], conv_w_out[i], conv_b_out[i])
            op = conv_module(hp, *cp, grid=False)
            os_ = conv_module(hs, *cp, grid=True)
        else:
            s0_ctx = jnp.zeros((hp.shape[0], 2, RET_HEADS, RET_DK, RET_DV), jnp.float32)
            op, st = retention_layer(hp, ret_w_in[i], ret_decay[i], ret_gn_g[i], ret_w_out[i], s0_ctx, rope=False)
            new_states.append(st)
            os_, _ = retention_layer(hs, ret_w_in[i], ret_decay[i], ret_gn_g[i], ret_w_out[i], state_ret[:, i], rope=True)
        xp = xp + g1p * op
        xs = xs + g1s * os_
        hp = modulate(xp, norm_ffn_g[l], sh2p, sc2p)
        hs = modulate(xs, norm_ffn_g[l], sh2s, sc2s)
        xp = xp + g2p * peer(hp, peer_w_q[l], peer_keys[l], peer_u[l], peer_v[l])
        xs = xs + g2s * peer(hs, peer_w_q[l], peer_keys[l], peer_u[l], peer_v[l])
    y_prompt = rmsnorm(xp, final_norm_g)
    y_sample = rmsnorm(xs, final_norm_g)
    new_state_ret = jnp.stack(new_states, axis=1)
    return (y_prompt, y_sample, new_state_ret)
```

```python
import functools
import math

import jax
import jax.numpy as jnp
from jax import lax
from jax.experimental import pallas as pl
from jax.experimental.pallas import tpu as pltpu

F32 = jnp.float32
BF16 = jnp.bfloat16
HIGHEST = lax.Precision.HIGHEST

EPS = 1e-6
GRID_W = 64
CONV_W = 31
RET_HEADS = 8
RET_CHUNK = 128
ROPE_BASE = 10000.0
PEER_HEADS = 8
PEER_DK = 256
N_KEYS = 128
PEER_TOPK = 16
COND_ROWS = 8
VMEM_LIMIT = 56 * 1024 * 1024


def _cparams(*sem):
    return pltpu.CompilerParams(dimension_semantics=sem, vmem_limit_bytes=VMEM_LIMIT)


def _cond_row(i, tm, t_prompt, dec_seq):
    t0 = i * tm
    return jnp.where(t0 < t_prompt, 0, 1 + (t0 - t_prompt) // dec_seq)


def _mod_spec(chunk, tm, t_prompt, dec_seq, width, ncol=1):
    nblk = (ncol,)

    def idx(i, *rest):
        j = rest[0] if (rest and ncol > 1) else 0
        return (_cond_row(i, tm, t_prompt, dec_seq), 0, chunk * nblk[0] + j)

    return pl.BlockSpec((None, 1, width), idx)


def _ada_kernel(cond_ref, w_ref, b_ref, o_ref):
    c = cond_ref[...]
    s = c * jax.nn.sigmoid(c)
    o_ref[...] = jnp.dot(s, w_ref[...], precision=HIGHEST, preferred_element_type=F32) + b_ref[...]


def _ada_table(cond, w_mod, b_mod):
    depth, d, n = w_mod.shape
    tn = 1024
    return pl.pallas_call(
        _ada_kernel,
        grid=(depth, n // tn),
        in_specs=[
            pl.BlockSpec((COND_ROWS, d), lambda l, j: (0, 0)),
            pl.BlockSpec((None, d, tn), lambda l, j: (l, 0, j)),
            pl.BlockSpec((None, 1, tn), lambda l, j: (l, 0, j)),
        ],
        out_specs=pl.BlockSpec((None, COND_ROWS, tn), lambda l, j: (l, 0, j)),
        out_shape=jax.ShapeDtypeStruct((depth, COND_ROWS, n), F32),
        compiler_params=_cparams("parallel", "parallel"),
        name="ada_table",
    )(cond, w_mod, b_mod.reshape(depth, 1, n))


def _modulated(x, g, sh, sc):
    ms = jnp.mean(x * x, axis=-1, keepdims=True)
    y = x * lax.rsqrt(ms + EPS) * g
    return y * (1.0 + sc) + sh


def _modmm_kernel(*refs, n_w, has_bias, epilogue):
    x_ref, g_ref, sh_ref, sc_ref = refs[:4]
    w_refs = refs[4:4 + n_w]
    b_refs = refs[4 + n_w:4 + n_w * (2 if has_bias else 1)]
    o_ref, h_ref = refs[-2], refs[-1]

    @pl.when(pl.program_id(1) == 0)
    def _():
        h_ref[...] = _modulated(x_ref[...], g_ref[...], sh_ref[...], sc_ref[...]).astype(BF16)

    h = h_ref[...]
    accs = [jnp.dot(h, w[...], preferred_element_type=F32) for w in w_refs]
    if has_bias:
        accs = [a + b[...] for a, b in zip(accs, b_refs)]
    o_ref[...] = epilogue(*accs).astype(o_ref.dtype)


def _modmm(x, g, mod3, chunks, w, w_col_blocks, n_out, out_dtype, epilogue, bias=None, *,
           t_prompt, dec_seq, tm=512, tn=1024, name):
    t, d = x.shape
    nj = n_out // tn
    n_w = len(w_col_blocks)
    in_specs = [
        pl.BlockSpec((tm, d), lambda i, j: (i, 0)),
        pl.BlockSpec((1, d), lambda i, j: (0, 0)),
        _mod_spec(chunks[0], tm, t_prompt, dec_seq, d),
        _mod_spec(chunks[1], tm, t_prompt, dec_seq, d),
    ]
    args = [x, g.reshape(1, d), mod3, mod3]
    for c in w_col_blocks:
        in_specs.append(pl.BlockSpec((d, tn), lambda i, j, c=c: (0, c * nj + j)))
        args.append(w)
    if bias is not None:
        b2 = bias.reshape(1, -1)
        for c in w_col_blocks:
            in_specs.append(pl.BlockSpec((1, tn), lambda i, j, c=c: (0, c * nj + j)))
            args.append(b2)
    return pl.pallas_call(
        functools.partial(_modmm_kernel, n_w=n_w, has_bias=bias is not None, epilogue=epilogue),
        grid=(t // tm, nj),
        in_specs=in_specs,
        out_specs=pl.BlockSpec((tm, tn), lambda i, j: (i, j)),
        out_shape=jax.ShapeDtypeStruct((t, n_out), out_dtype),
        scratch_shapes=[pltpu.VMEM((tm, d), BF16)],
        compiler_params=_cparams("parallel", "arbitrary"),
        name=name,
    )(*args)


def _mmres_kernel(a_ref, w_ref, b_ref, x_ref, gate_ref, o_ref):
    y = jnp.dot(a_ref[...], w_ref[...], preferred_element_type=F32) + b_ref[...]
    o_ref[...] = x_ref[...] + gate_ref[...] * y


def _mm_residual(a, w, bias, x, mod3, gate_chunk, *, t_prompt, dec_seq, tm=512, tn=1024, name):
    t, k = a.shape
    d = x.shape[1]
    nj = d // tn
    return pl.pallas_call(
        _mmres_kernel,
        grid=(t // tm, nj),
        in_specs=[
            pl.BlockSpec((tm, k), lambda i, j: (i, 0)),
            pl.BlockSpec((k, tn), lambda i, j: (0, j)),
            pl.BlockSpec((1, tn), lambda i, j: (0, j)),
            pl.BlockSpec((tm, tn), lambda i, j: (i, j)),
            _mod_spec(gate_chunk, tm, t_prompt, dec_seq, tn, ncol=nj),
        ],
        out_specs=pl.BlockSpec((tm, tn), lambda i, j: (i, j)),
        out_shape=jax.ShapeDtypeStruct((t, d), F32),
        compiler_params=_cparams("parallel", "arbitrary"),
        name=name,
    )(a, w, bias.reshape(1, d), x, mod3)


CONV_SEG = GRID_W
CONV_HALO = 16
CONV_BLOCK = 256


def _dwconv_kernel(a_ref, dw_ref, dwb_ref, ng_ref, o_ref, buf_ref, c_ref, *, joined_blocks):
    nseg = CONV_BLOCK // CONV_SEG
    d = a_ref.shape[1]
    joined = (pl.program_id(0) < joined_blocks).astype(F32)
    zeros = jnp.zeros((CONV_HALO, d), F32)
    for s in range(nseg):
        lo = s * CONV_SEG
        buf_ref[s, CONV_HALO:CONV_HALO + CONV_SEG, :] = a_ref[lo:lo + CONV_SEG, :]
        if s == 0:
            buf_ref[s, 0:CONV_HALO, :] = zeros
        else:
            buf_ref[s, 0:CONV_HALO, :] = a_ref[lo - CONV_HALO:lo, :] * joined
        if s == nseg - 1:
            buf_ref[s, CONV_HALO + CONV_SEG:, :] = zeros
        else:
            buf_ref[s, CONV_HALO + CONV_SEG:, :] = a_ref[lo + CONV_SEG:lo + CONV_SEG + CONV_HALO, :] * joined

    lane = 256
    base = CONV_HALO - CONV_W // 2
    for s in range(nseg):
        for c in range(d // lane):
            ls = slice(c * lane, (c + 1) * lane)
            acc = jnp.zeros((CONV_SEG, lane), F32) + dwb_ref[:, ls]
            for k in range(CONV_W):
                acc = acc + buf_ref[s, base + k:base + k + CONV_SEG, ls] * dw_ref[k:k + 1, ls]
            c_ref[s * CONV_SEG:(s + 1) * CONV_SEG, ls] = acc

    y = c_ref[...]
    ms = jnp.mean(y * y, axis=-1, keepdims=True)
    y = y * lax.rsqrt(ms + EPS) * ng_ref[...]
    o_ref[...] = (y * jax.nn.sigmoid(y)).astype(o_ref.dtype)


def _dwconv(a, dw, dw_b, norm_g, *, t_prompt, seq):
    t, d = a.shape
    assert seq == CONV_BLOCK
    nseg = CONV_BLOCK // CONV_SEG
    return pl.pallas_call(
        functools.partial(_dwconv_kernel, joined_blocks=t_prompt // CONV_BLOCK),
        grid=(t // CONV_BLOCK,),
        in_specs=[
            pl.BlockSpec((CONV_BLOCK, d), lambda i: (i, 0)),
            pl.BlockSpec((CONV_W, d), lambda i: (0, 0)),
            pl.BlockSpec((1, d), lambda i: (0, 0)),
            pl.BlockSpec((1, d), lambda i: (0, 0)),
        ],
        out_specs=pl.BlockSpec((CONV_BLOCK, d), lambda i: (i, 0)),
        out_shape=jax.ShapeDtypeStruct((t, d), BF16),
        scratch_shapes=[
            pltpu.VMEM((nseg, CONV_SEG + 2 * CONV_HALO, d), F32),
            pltpu.VMEM((CONV_BLOCK, d), F32),
        ],
        compiler_params=_cparams("parallel"),
        name="dwconv",
    )(a, dw, dw_b.reshape(1, d), norm_g.reshape(1, d))


def _swap_halves(x):
    parts = [pltpu.roll(x[:, g * 128:(g + 1) * 128], 64, axis=1) for g in range(x.shape[1] // 128)]
    return jnp.concatenate(parts, axis=1) if len(parts) > 1 else parts[0]


def _ret_kernel(*refs, backward, rope, has_s0, emit_state, finish, n_chunks):
    it = iter(refs)
    lg_ref, q_ref, k_ref, v_ref = next(it), next(it), next(it), next(it)
    cos_ref, sin_ref = (next(it), next(it)) if rope else (None, None)
    s0_ref = next(it) if has_s0 else None
    op_ref, gate_ref, gn_ref = (next(it), next(it), next(it)) if finish else (None, None, None)
    o_ref = next(it)
    st_ref = next(it) if emit_state else None
    s_ref = next(it)

    c = pl.program_id(1)
    C = q_ref.shape[0]
    dk = q_ref.shape[1] // RET_HEADS
    dv = v_ref.shape[1] // RET_HEADS
    direction = 1 if backward else 0

    @pl.when(c == 0)
    def _():
        if has_s0:
            s_ref[...] = s0_ref[...]
        else:
            s_ref[...] = jnp.zeros_like(s_ref)

    row = lax.broadcasted_iota(jnp.int32, (C, C), 0)
    col = lax.broadcasted_iota(jnp.int32, (C, C), 1)
    rel = (col - row) if backward else (row - col)
    relf = jnp.maximum(rel, 0).astype(F32)
    pos = lax.broadcasted_iota(jnp.int32, (C, dk), 0)
    xi_e = ((C - pos) if backward else (pos + 1)).astype(F32)
    zeta_e = (pos if backward else (C - 1 - pos)).astype(F32)

    for h in range(RET_HEADS):
        lg = lg_ref[direction, h]
        q = q_ref[:, h * dk:(h + 1) * dk].astype(F32)
        k = k_ref[:, h * dk:(h + 1) * dk].astype(F32) * (dk ** -0.5)
        if rope:
            cs, sn = cos_ref[...], sin_ref[...]
            q = q * cs + _swap_halves(q) * sn
            k = k * cs + _swap_halves(k) * sn
        v = v_ref[:, h * dv:(h + 1) * dv]
        decay = jnp.where(rel >= 0, jnp.exp(lg * relf), 0.0)
        scores = lax.dot_general(q.astype(BF16), k.astype(BF16), (((1,), (1,)), ((), ())),
                                 preferred_element_type=F32) * decay
        inner = jnp.dot(scores.astype(BF16), v, preferred_element_type=F32)
        s_old = s_ref[h]
        cross = jnp.dot((q * jnp.exp(lg * xi_e)).astype(BF16), s_old.astype(BF16),
                        preferred_element_type=F32)
        kz_t = (k * jnp.exp(lg * zeta_e)).T.astype(BF16)
        s_new = jnp.exp(lg * C) * s_old + jnp.dot(kz_t, v, preferred_element_type=F32)
        s_ref[h] = s_new
        o = inner + cross
        if finish:
            o = o + op_ref[:, h * dv:(h + 1) * dv]
            mu = jnp.mean(o, axis=-1, keepdims=True)
            oc = o - mu
            var = jnp.mean(oc * oc, axis=-1, keepdims=True)
            o = oc * lax.rsqrt(var + EPS) * gn_ref[:, h * dv:(h + 1) * dv]
            g = gate_ref[:, h * dv:(h + 1) * dv].astype(F32)
            o = g * jax.nn.sigmoid(g) * o
        o_ref[:, h * dv:(h + 1) * dv] = o.astype(o_ref.dtype)

    if emit_state:
        @pl.when(c == n_chunks - 1)
        def _():
            st_ref[...] = s_ref[...]


def _ret_sweep(proj, log_gamma, *, tok0, batch, seq, backward, rope_tabs=None, s0=None,
               o_prev=None, gn_g=None, emit_state=False, out_dtype=F32, name):
    C = RET_CHUNK
    d = proj.shape[1] // 6
    dv_all = 2 * d
    n_chunks = seq // C
    blk0 = tok0 // C
    finish = o_prev is not None

    def chunk(c):
        return (n_chunks - 1 - c) if backward else c

    def tok_blk(b, c):
        return blk0 + b * n_chunks + chunk(c)

    def out_blk(b, c):
        return b * n_chunks + chunk(c)

    in_specs = [
        pl.BlockSpec(memory_space=pltpu.SMEM),
        pl.BlockSpec((C, d), lambda b, c: (tok_blk(b, c), 0)),
        pl.BlockSpec((C, d), lambda b, c: (tok_blk(b, c), 1)),
        pl.BlockSpec((C, dv_all), lambda b, c: (tok_blk(b, c), 1)),
    ]
    args = [log_gamma, proj, proj, proj]
    if rope_tabs is not None:
        dk = d // RET_HEADS
        in_specs += [pl.BlockSpec((C, dk), lambda b, c: (chunk(c), 0))] * 2
        args += list(rope_tabs)
    if s0 is not None:
        direction = 1 if backward else 0
        in_specs.append(pl.BlockSpec((None, None) + s0.shape[2:], lambda b, c: (b, direction, 0, 0, 0)))
        args.append(s0)
    if finish:
        in_specs += [
            pl.BlockSpec((C, dv_all), lambda b, c: (out_blk(b, c), 0)),
            pl.BlockSpec((C, dv_all), lambda b, c: (tok_blk(b, c), 2)),
            pl.BlockSpec((1, dv_all), lambda b, c: (0, 0)),
        ]
        args += [o_prev, proj, gn_g.reshape(1, dv_all)]
    state_shape = (RET_HEADS, d // RET_HEADS, dv_all // RET_HEADS)
    out_specs = [pl.BlockSpec((C, dv_all), lambda b, c: (out_blk(b, c), 0))]
    out_shape = [jax.ShapeDtypeStruct((batch * seq, dv_all), out_dtype)]
    if emit_state:
        out_specs.append(pl.BlockSpec((None,) + state_shape, lambda b, c: (b, 0, 0, 0)))
        out_shape.append(jax.ShapeDtypeStruct((batch,) + state_shape, F32))
    res = pl.pallas_call(
        functools.partial(_ret_kernel, backward=backward, rope=rope_tabs is not None,
                          has_s0=s0 is not None, emit_state=emit_state, finish=finish,
                          n_chunks=n_chunks),
        grid=(batch, n_chunks),
        in_specs=in_specs,
        out_specs=out_specs,
        out_shape=out_shape,
        scratch_shapes=[pltpu.VMEM(state_shape, F32)],
        compiler_params=_cparams("parallel", "arbitrary"),
        name=name,
    )(*args)
    return res if emit_state else (res[0], None)


def _rope_tables(seq, dk):
    t = jnp.arange(seq)
    rowp = (t // GRID_W).astype(F32)
    colp = (t % GRID_W).astype(F32)
    nf = dk // 4
    freqs = ROPE_BASE ** (-jnp.arange(nf, dtype=F32) / nf)
    ar = rowp[:, None] * freqs[None, :]
    ac = colp[:, None] * freqs[None, :]
    cos = jnp.concatenate([jnp.cos(ar), jnp.cos(ar), jnp.cos(ac), jnp.cos(ac)], axis=-1)
    sin = jnp.concatenate([-jnp.sin(ar), jnp.sin(ar), -jnp.sin(ac), jnp.sin(ac)], axis=-1)
    return cos, sin


def _top_values(work, out_ref, h, k):
    for r in range(k):
        m = jnp.max(work, axis=0, keepdims=True)
        out_ref[r, h:h + 1, :] = m
        if r + 1 < k:
            work = jnp.where(work == m, -jnp.inf, work)


def _max_tree(xs):
    xs = list(xs)
    while len(xs) > 1:
        nxt = [jnp.maximum(xs[i], xs[i + 1]) for i in range(0, len(xs) - 1, 2)]
        if len(xs) % 2:
            nxt.append(xs[-1])
        xs = nxt
    return xs[0]


def _peer_sel_kernel(x_ref, g_ref, sh_ref, sc_ref, wqt_ref, k0_ref, k1_ref,
                     xt_ref, s1_ref, s2_ref, e1_ref, e2_ref, tau_ref,
                     qt_ref, v1_ref, v2_ref):
    K = PEER_TOPK
    half = PEER_DK // 2
    ht = _modulated(x_ref[...], g_ref[...], sh_ref[...], sc_ref[...]).T.astype(BF16)
    xt_ref[...] = ht
    qt_ref[...] = jnp.dot(wqt_ref[...], ht, preferred_element_type=F32)
    for h in range(PEER_HEADS):
        q1 = qt_ref[h * PEER_DK:h * PEER_DK + half, :]
        q2 = qt_ref[h * PEER_DK + half:(h + 1) * PEER_DK, :]
        s1 = jnp.dot(k0_ref[...], q1, precision=HIGHEST, preferred_element_type=F32)
        s2 = jnp.dot(k1_ref[...], q2, precision=HIGHEST, preferred_element_type=F32)
        s1_ref[h] = s1
        s2_ref[h] = s2
        _top_values(s1, v1_ref, h, K)
        _top_values(s2, v2_ref, h, K)

    v1 = [v1_ref[r] for r in range(K)]
    v2 = [v2_ref[r] for r in range(K)]
    cands = [v1[a] + v2[b] for a in range(K) for b in range(K) if (a + 1) * (b + 1) <= K]
    top = v1[0] + v2[0]
    z = jnp.zeros_like(top)
    mx = top
    for r in range(K):
        mx = _max_tree(cands)
        z = z + jnp.exp(mx - top)
        if r + 1 < K:
            cands = [jnp.where(cd == mx, -jnp.inf, cd) for cd in cands]
    tau_ref[...] = mx
    inv_z = 1.0 / z
    for h in range(PEER_HEADS):
        e1_ref[h] = jnp.exp(s1_ref[h] - v1[0][h:h + 1, :])
        e2_ref[h] = jnp.exp(s2_ref[h] - v2[0][h:h + 1, :]) * inv_z[h:h + 1, :]


def _peer_select(x, g, mod3, wq_t, keys, *, t_prompt, dec_seq, tt=256):
    t, d = x.shape
    fac = jax.ShapeDtypeStruct((PEER_HEADS, N_KEYS, t), F32)
    fac_spec = pl.BlockSpec((PEER_HEADS, N_KEYS, tt), lambda i: (0, 0, i))
    return pl.pallas_call(
        _peer_sel_kernel,
        grid=(t // tt,),
        in_specs=[
            pl.BlockSpec((tt, d), lambda i: (i, 0)),
            pl.BlockSpec((1, d), lambda i: (0, 0)),
            _mod_spec(3, tt, t_prompt, dec_seq, d),
            _mod_spec(4, tt, t_prompt, dec_seq, d),
            pl.BlockSpec((PEER_HEADS * PEER_DK, d), lambda i: (0, 0)),
            pl.BlockSpec((N_KEYS, PEER_DK // 2), lambda i: (0, 0)),
            pl.BlockSpec((N_KEYS, PEER_DK // 2), lambda i: (0, 0)),
        ],
        out_specs=[
            pl.BlockSpec((d, tt), lambda i: (0, i)),
            fac_spec, fac_spec, fac_spec, fac_spec,
            pl.BlockSpec((PEER_HEADS, tt), lambda i: (0, i)),
        ],
        out_shape=[jax.ShapeDtypeStruct((d, t), BF16), fac, fac, fac, fac,
                   jax.ShapeDtypeStruct((PEER_HEADS, t), F32)],
        scratch_shapes=[
            pltpu.VMEM((PEER_HEADS * PEER_DK, tt), F32),
            pltpu.VMEM((PEER_TOPK, PEER_HEADS, tt), F32),
            pltpu.VMEM((PEER_TOPK, PEER_HEADS, tt), F32),
        ],
        compiler_params=_cparams("parallel"),
        name="peer_select",
    )(x, g.reshape(1, d), mod3, mod3, wq_t, keys[0], keys[1])


def _peer_dense_kernel(xt_ref, u_ref, vt_ref, s1g_ref, e1g_ref, s2_ref, e2_ref, tau_ref,
                       x_ref, gate_ref, fg_ref, o_ref, acc_ref, a_ref, w_ref, *, n_tiles, final_norm):
    j = pl.program_id(1)
    te, tt = a_ref.shape
    ng = te // N_KEYS

    @pl.when(j == 0)
    def _():
        acc_ref[...] = jnp.zeros_like(acc_ref)

    a_ref[...] = jnp.dot(u_ref[...], xt_ref[...], preferred_element_type=F32)

    for g in range(ng):
        rows = slice(g * N_KEYS, (g + 1) * N_KEYS)
        for c in range(tt // 128):
            ls = slice(c * 128, (c + 1) * 128)
            gsum = jnp.zeros((N_KEYS, 128), F32)
            for h in range(PEER_HEADS):
                sm = s1g_ref[h, g:g + 1, ls] + s2_ref[h, :, ls]
                val = e1g_ref[h, g:g + 1, ls] * e2_ref[h, :, ls]
                gsum = gsum + jnp.where(sm >= tau_ref[h, :, ls], val, 0.0)
            w_ref[rows, ls] = (gsum * jax.nn.gelu(a_ref[rows, ls])).astype(BF16)

    acc_ref[...] += jnp.dot(vt_ref[...], w_ref[...], preferred_element_type=F32)

    @pl.when(j == n_tiles - 1)
    def _():
        y = x_ref[...] + gate_ref[...] * acc_ref[...].T
        if final_norm:
            ms = jnp.mean(y * y, axis=-1, keepdims=True)
            y = y * lax.rsqrt(ms + EPS) * fg_ref[...]
        o_ref[...] = y


def _peer_dense(xt, u, vt, s1, s2, e1, e2, tau, x, mod3, final_g, *, final_norm,
                t_prompt, dec_seq, tt=512, te=1024):
    t, d = x.shape
    n_exp = u.shape[0]
    n_tiles = n_exp // te
    ng = te // N_KEYS
    once = pl.Buffered(1)
    fac_spec = pl.BlockSpec((PEER_HEADS, N_KEYS, tt), lambda i, j: (0, 0, i), pipeline_mode=once)
    row_spec = pl.BlockSpec((PEER_HEADS, ng, tt), lambda i, j: (0, j, i))
    return pl.pallas_call(
        functools.partial(_peer_dense_kernel, n_tiles=n_tiles, final_norm=final_norm),
        grid=(t // tt, n_tiles),
        in_specs=[
            pl.BlockSpec((d, tt), lambda i, j: (0, i), pipeline_mode=once),
            pl.BlockSpec((te, d), lambda i, j: (j, 0)),
            pl.BlockSpec((d, te), lambda i, j: (0, j)),
            row_spec, row_spec, fac_spec, fac_spec,
            pl.BlockSpec((PEER_HEADS, 1, tt), lambda i, j: (0, 0, i), pipeline_mode=once),
            pl.BlockSpec((tt, d), lambda i, j: (i, 0), pipeline_mode=once),
            _mod_spec(5, tt, t_prompt, dec_seq, d),
            pl.BlockSpec((1, d), lambda i, j: (0, 0)),
        ],
        out_specs=pl.BlockSpec((tt, d), lambda i, j: (i, 0)),
        out_shape=jax.ShapeDtypeStruct((t, d), F32),
        scratch_shapes=[
            pltpu.VMEM((d, tt), F32),
            pltpu.VMEM((te, tt), F32),
            pltpu.VMEM((te, tt), BF16),
        ],
        compiler_params=_cparams("parallel", "arbitrary"),
        name="peer_dense",
    )(xt, u, vt, s1, e1, s2, e2, tau.reshape(PEER_HEADS, 1, t), x, mod3, final_g.reshape(1, d))


def _peer_layer(x, norm_g, mod3, w_q, keys, u, v, final_g, *, final_norm, t_prompt, dec_seq):
    wq_t = w_q.T.astype(BF16)
    xt, s1, s2, e1, e2, tau = _peer_select(x, norm_g, mod3, wq_t, keys, t_prompt=t_prompt, dec_seq=dec_seq)
    return _peer_dense(xt, u.astype(BF16), v.T.astype(BF16), s1, s2, e1, e2, tau, x, mod3, final_g,
                       final_norm=final_norm, t_prompt=t_prompt, dec_seq=dec_seq)


def _glu(val, gate):
    return val * jax.nn.sigmoid(gate)


def _identity(y):
    return y


def kernel(x_prompt, x_sample, state_ret, c, c_ctx, w_mod, b_mod, norm_mix_g, norm_ffn_g, conv_w_in, conv_b_in, conv_dw, conv_dw_b, conv_norm_g, conv_w_out, conv_b_out, ret_w_in, ret_decay, ret_gn_g, ret_w_out, peer_w_q, peer_keys, peer_u, peer_v, final_norm_g):
    batch, seq, d = x_prompt.shape
    dec_batch, dec_seq, _ = x_sample.shape
    depth = w_mod.shape[0]
    t_prompt = batch * seq
    t_sample = dec_batch * dec_seq
    sizes = dict(t_prompt=t_prompt, dec_seq=dec_seq)

    x = jnp.concatenate([x_prompt.reshape(t_prompt, d), x_sample.reshape(t_sample, d)], axis=0)
    cond = jnp.zeros((COND_ROWS, d), F32).at[0].set(c_ctx).at[1:1 + dec_batch].set(c)
    mod = _ada_table(cond, w_mod, b_mod)

    new_states = []
    for l in range(depth):
        mod3 = mod[l].reshape(COND_ROWS, 1, 6 * d)
        i = l // 2
        if l % 2 == 0:
            a = _modmm(x, norm_mix_g[l], mod3, (0, 1), conv_w_in[i].astype(BF16), (0, 1), d, F32, _glu,
                       bias=conv_b_in[i], name="conv_in", **sizes)
            cv = _dwconv(a, conv_dw[i], conv_dw_b[i], conv_norm_g[i], t_prompt=t_prompt, seq=seq)
            x = _mm_residual(cv, conv_w_out[i].astype(BF16), conv_b_out[i], x, mod3, 2,
                             name="conv_out", **sizes)
        else:
            proj = _modmm(x, norm_mix_g[l], mod3, (0, 1), ret_w_in[i].astype(BF16), (0,), 6 * d, BF16,
                          _identity, name="ret_in", **sizes)
            log_gamma = jax.nn.log_sigmoid(ret_decay[i].astype(F32))
            of, sf = _ret_sweep(proj, log_gamma, tok0=0, batch=batch, seq=seq, backward=False,
                                emit_state=True, name="ret_ctx_fwd")
            yp, sb = _ret_sweep(proj, log_gamma, tok0=0, batch=batch, seq=seq, backward=True,
                                o_prev=of, gn_g=ret_gn_g[i], emit_state=True, out_dtype=BF16,
                                name="ret_ctx_bwd")
            new_states.append(jnp.stack([sf, sb], axis=1))
            tabs = _rope_tables(dec_seq, d // RET_HEADS)
            s0 = state_ret[:, i]
            of, _ = _ret_sweep(proj, log_gamma, tok0=t_prompt, batch=dec_batch, seq=dec_seq,
                               backward=False, rope_tabs=tabs, s0=s0, name="ret_lat_fwd")
            ys, _ = _ret_sweep(proj, log_gamma, tok0=t_prompt, batch=dec_batch, seq=dec_seq,
                               backward=True, rope_tabs=tabs, s0=s0, o_prev=of, gn_g=ret_gn_g[i],
                               out_dtype=BF16, name="ret_lat_bwd")
            y = jnp.concatenate([yp, ys], axis=0)
            x = _mm_residual(y, ret_w_out[i].astype(BF16), jnp.zeros((d,), F32), x, mod3, 2,
                             name="ret_out", **sizes)
        x = _peer_layer(x, norm_ffn_g[l], mod3, peer_w_q[l], peer_keys[l], peer_u[l], peer_v[l],
                        final_norm_g, final_norm=(l == depth - 1), **sizes)

    y_prompt = x[:t_prompt].reshape(batch, seq, d)
    y_sample = x[t_prompt:].reshape(dec_batch, dec_seq, d)
    return (y_prompt, y_sample, jnp.stack(new_states, axis=1))
```

```python
import functools
import math

import jax
import jax.numpy as jnp
from jax import lax
from jax.experimental import pallas as pl
from jax.experimental.pallas import tpu as pltpu

F32 = jnp.float32
BF16 = jnp.bfloat16
HIGHEST = lax.Precision.HIGHEST

EPS = 1e-6
GRID_W = 64
CONV_W = 31
RET_HEADS = 8
RET_CHUNK = 128
ROPE_BASE = 10000.0
PEER_HEADS = 8
PEER_DK = 256
N_KEYS = 128
PEER_TOPK = 16
COND_ROWS = 8
VMEM_LIMIT = 56 * 1024 * 1024


def _cparams(*sem):
    return pltpu.CompilerParams(dimension_semantics=sem, vmem_limit_bytes=VMEM_LIMIT)


def _cond_row(i, tm, t_prompt, dec_seq):
    t0 = i * tm
    return jnp.where(t0 < t_prompt, 0, 1 + (t0 - t_prompt) // dec_seq)


def _mod_spec(chunk, tm, t_prompt, dec_seq, width, ncol=1):
    nblk = (ncol,)

    def idx(i, *rest):
        j = rest[0] if (rest and ncol > 1) else 0
        return (_cond_row(i, tm, t_prompt, dec_seq), 0, chunk * nblk[0] + j)

    return pl.BlockSpec((None, 1, width), idx)


def _ada_kernel(cond_ref, w_ref, b_ref, o_ref):
    c = cond_ref[...]
    s = c * jax.nn.sigmoid(c)
    o_ref[...] = jnp.dot(s, w_ref[...], precision=HIGHEST, preferred_element_type=F32) + b_ref[...]


def _ada_table(cond, w_mod, b_mod):
    depth, d, n = w_mod.shape
    tn = 1024
    return pl.pallas_call(
        _ada_kernel,
        grid=(depth, n // tn),
        in_specs=[
            pl.BlockSpec((COND_ROWS, d), lambda l, j: (0, 0)),
            pl.BlockSpec((None, d, tn), lambda l, j: (l, 0, j)),
            pl.BlockSpec((None, 1, tn), lambda l, j: (l, 0, j)),
        ],
        out_specs=pl.BlockSpec((None, COND_ROWS, tn), lambda l, j: (l, 0, j)),
        out_shape=jax.ShapeDtypeStruct((depth, COND_ROWS, n), F32),
        compiler_params=_cparams("parallel", "parallel"),
        name="ada_table",
    )(cond, w_mod, b_mod.reshape(depth, 1, n))


def _modulated(x, g, sh, sc):
    ms = jnp.mean(x * x, axis=-1, keepdims=True)
    y = x * lax.rsqrt(ms + EPS) * g
    return y * (1.0 + sc) + sh


def _modmm_kernel(*refs, n_w, has_bias, epilogue):
    x_ref, g_ref, sh_ref, sc_ref = refs[:4]
    w_refs = refs[4:4 + n_w]
    b_refs = refs[4 + n_w:4 + n_w * (2 if has_bias else 1)]
    o_ref, h_ref = refs[-2], refs[-1]

    @pl.when(pl.program_id(1) == 0)
    def _():
        h_ref[...] = _modulated(x_ref[...], g_ref[...], sh_ref[...], sc_ref[...]).astype(BF16)

    h = h_ref[...]
    accs = [jnp.dot(h, w[...], preferred_element_type=F32) for w in w_refs]
    if has_bias:
        accs = [a + b[...] for a, b in zip(accs, b_refs)]
    o_ref[...] = epilogue(*accs).astype(o_ref.dtype)


def _modmm(x, g, mod3, chunks, w, w_col_blocks, n_out, out_dtype, epilogue, bias=None, *,
           t_prompt, dec_seq, tm=512, tn=1024, name):
    t, d = x.shape
    nj = n_out // tn
    n_w = len(w_col_blocks)
    in_specs = [
        pl.BlockSpec((tm, d), lambda i, j: (i, 0)),
        pl.BlockSpec((1, d), lambda i, j: (0, 0)),
        _mod_spec(chunks[0], tm, t_prompt, dec_seq, d),
        _mod_spec(chunks[1], tm, t_prompt, dec_seq, d),
    ]
    args = [x, g.reshape(1, d), mod3, mod3]
    for c in w_col_blocks:
        in_specs.append(pl.BlockSpec((d, tn), lambda i, j, c=c: (0, c * nj + j)))
        args.append(w)
    if bias is not None:
        b2 = bias.reshape(1, -1)
        for c in w_col_blocks:
            in_specs.append(pl.BlockSpec((1, tn), lambda i, j, c=c: (0, c * nj + j)))
            args.append(b2)
    return pl.pallas_call(
        functools.partial(_modmm_kernel, n_w=n_w, has_bias=bias is not None, epilogue=epilogue),
        grid=(t // tm, nj),
        in_specs=in_specs,
        out_specs=pl.BlockSpec((tm, tn), lambda i, j: (i, j)),
        out_shape=jax.ShapeDtypeStruct((t, n_out), out_dtype),
        scratch_shapes=[pltpu.VMEM((tm, d), BF16)],
        compiler_params=_cparams("parallel", "arbitrary"),
        name=name,
    )(*args)


def _mmres_kernel(a_ref, w_ref, b_ref, x_ref, gate_ref, o_ref):
    y = jnp.dot(a_ref[...], w_ref[...], preferred_element_type=F32) + b_ref[...]
    o_ref[...] = x_ref[...] + gate_ref[...] * y


def _mm_residual(a, w, bias, x, mod3, gate_chunk, *, t_prompt, dec_seq, tm=512, tn=1024, name):
    t, k = a.shape
    d = x.shape[1]
    nj = d // tn
    return pl.pallas_call(
        _mmres_kernel,
        grid=(t // tm, nj),
        in_specs=[
            pl.BlockSpec((tm, k), lambda i, j: (i, 0)),
            pl.BlockSpec((k, tn), lambda i, j: (0, j)),
            pl.BlockSpec((1, tn), lambda i, j: (0, j)),
            pl.BlockSpec((tm, tn), lambda i, j: (i, j)),
            _mod_spec(gate_chunk, tm, t_prompt, dec_seq, tn, ncol=nj),
        ],
        out_specs=pl.BlockSpec((tm, tn), lambda i, j: (i, j)),
        out_shape=jax.ShapeDtypeStruct((t, d), F32),
        compiler_params=_cparams("parallel", "arbitrary"),
        name=name,
    )(a, w, bias.reshape(1, d), x, mod3)


CONV_SEG = GRID_W
CONV_HALO = 16
CONV_BLOCK = 256


def _dwconv_kernel(a_ref, dw_ref, dwb_ref, ng_ref, o_ref, buf_ref, c_ref, win_ref, *, joined_blocks):
    nseg = CONV_BLOCK // CONV_SEG
    d = a_ref.shape[1]
    joined = (pl.program_id(0) < joined_blocks).astype(F32)
    zeros = jnp.zeros((CONV_HALO, d), F32)
    for s in range(nseg):
        lo = s * CONV_SEG
        buf_ref[s, CONV_HALO:CONV_HALO + CONV_SEG, :] = a_ref[lo:lo + CONV_SEG, :]
        if s == 0:
            buf_ref[s, 0:CONV_HALO, :] = zeros
        else:
            buf_ref[s, 0:CONV_HALO, :] = a_ref[lo - CONV_HALO:lo, :] * joined
        if s == nseg - 1:
            buf_ref[s, CONV_HALO + CONV_SEG:, :] = zeros
        else:
            buf_ref[s, CONV_HALO + CONV_SEG:, :] = a_ref[lo + CONV_SEG:lo + CONV_SEG + CONV_HALO, :] * joined

    lane = 256
    sub = 8
    base = CONV_HALO - CONV_W // 2
    span = CONV_SEG + 2 * CONV_HALO - sub
    for s in range(nseg):
        for c in range(d // lane):
            ls = slice(c * lane, (c + 1) * lane)
            acc = jnp.zeros((CONV_SEG, lane), F32) + dwb_ref[:, ls]
            for r in range(sub):
                win_ref[...] = buf_ref[s, r:r + span, ls]
                for k in range(CONV_W):
                    if (base + k) % sub == r:
                        q = (base + k) // sub * sub
                        acc = acc + win_ref[q:q + CONV_SEG, :] * dw_ref[k:k + 1, ls]
            c_ref[s * CONV_SEG:(s + 1) * CONV_SEG, ls] = acc

    y = c_ref[...]
    ms = jnp.mean(y * y, axis=-1, keepdims=True)
    y = y * lax.rsqrt(ms + EPS) * ng_ref[...]
    o_ref[...] = (y * jax.nn.sigmoid(y)).astype(o_ref.dtype)


def _dwconv(a, dw, dw_b, norm_g, *, t_prompt, seq):
    t, d = a.shape
    assert seq == CONV_BLOCK
    nseg = CONV_BLOCK // CONV_SEG
    return pl.pallas_call(
        functools.partial(_dwconv_kernel, joined_blocks=t_prompt // CONV_BLOCK),
        grid=(t // CONV_BLOCK,),
        in_specs=[
            pl.BlockSpec((CONV_BLOCK, d), lambda i: (i, 0)),
            pl.BlockSpec((CONV_W, d), lambda i: (0, 0)),
            pl.BlockSpec((1, d), lambda i: (0, 0)),
            pl.BlockSpec((1, d), lambda i: (0, 0)),
        ],
        out_specs=pl.BlockSpec((CONV_BLOCK, d), lambda i: (i, 0)),
        out_shape=jax.ShapeDtypeStruct((t, d), BF16),
        scratch_shapes=[
            pltpu.VMEM((nseg, CONV_SEG + 2 * CONV_HALO, d), F32),
            pltpu.VMEM((CONV_BLOCK, d), F32),
            pltpu.VMEM((CONV_SEG + 2 * CONV_HALO - 8, 256), F32),
        ],
        compiler_params=_cparams("parallel"),
        name="dwconv",
    )(a, dw, dw_b.reshape(1, d), norm_g.reshape(1, d))


def _swap_halves(x):
    parts = [pltpu.roll(x[:, g * 128:(g + 1) * 128], 64, axis=1) for g in range(x.shape[1] // 128)]
    return jnp.concatenate(parts, axis=1) if len(parts) > 1 else parts[0]


def _ret_kernel(*refs, backward, rope, has_s0, emit_state, finish, n_chunks):
    it = iter(refs)
    lg_ref, q_ref, k_ref, v_ref = next(it), next(it), next(it), next(it)
    cos_ref, sin_ref = (next(it), next(it)) if rope else (None, None)
    s0_ref = next(it) if has_s0 else None
    op_ref, gate_ref, gn_ref = (next(it), next(it), next(it)) if finish else (None, None, None)
    o_ref = next(it)
    st_ref = next(it) if emit_state else None
    s_ref = next(it)

    c = pl.program_id(1)
    C = q_ref.shape[0]
    dk = q_ref.shape[1] // RET_HEADS
    dv = v_ref.shape[1] // RET_HEADS
    direction = 1 if backward else 0

    @pl.when(c == 0)
    def _():
        if has_s0:
            s_ref[...] = s0_ref[...]
        else:
            s_ref[...] = jnp.zeros_like(s_ref)

    row = lax.broadcasted_iota(jnp.int32, (C, C), 0)
    col = lax.broadcasted_iota(jnp.int32, (C, C), 1)
    rel = (col - row) if backward else (row - col)
    relf = jnp.maximum(rel, 0).astype(F32)
    pos = lax.broadcasted_iota(jnp.int32, (C, dk), 0)
    xi_e = ((C - pos) if backward else (pos + 1)).astype(F32)
    zeta_e = (pos if backward else (C - 1 - pos)).astype(F32)

    for h in range(RET_HEADS):
        lg = lg_ref[direction, h]
        q = q_ref[:, h * dk:(h + 1) * dk].astype(F32)
        k = k_ref[:, h * dk:(h + 1) * dk].astype(F32) * (dk ** -0.5)
        if rope:
            cs, sn = cos_ref[...], sin_ref[...]
            q = q * cs + _swap_halves(q) * sn
            k = k * cs + _swap_halves(k) * sn
        v = v_ref[:, h * dv:(h + 1) * dv]
        decay = jnp.where(rel >= 0, jnp.exp(lg * relf), 0.0)
        scores = lax.dot_general(q.astype(BF16), k.astype(BF16), (((1,), (1,)), ((), ())),
                                 preferred_element_type=F32) * decay
        inner = jnp.dot(scores.astype(BF16), v, preferred_element_type=F32)
        s_old = s_ref[h]
        cross = jnp.dot((q * jnp.exp(lg * xi_e)).astype(BF16), s_old.astype(BF16),
                        preferred_element_type=F32)
        kz_t = (k * jnp.exp(lg * zeta_e)).T.astype(BF16)
        s_new = jnp.exp(lg * C) * s_old + jnp.dot(kz_t, v, preferred_element_type=F32)
        s_ref[h] = s_new
        o = inner + cross
        if finish:
            o = o + op_ref[:, h * dv:(h + 1) * dv]
            mu = jnp.mean(o, axis=-1, keepdims=True)
            oc = o - mu
            var = jnp.mean(oc * oc, axis=-1, keepdims=True)
            o = oc * lax.rsqrt(var + EPS) * gn_ref[:, h * dv:(h + 1) * dv]
            g = gate_ref[:, h * dv:(h + 1) * dv].astype(F32)
            o = g * jax.nn.sigmoid(g) * o
        o_ref[:, h * dv:(h + 1) * dv] = o.astype(o_ref.dtype)

    if emit_state:
        @pl.when(c == n_chunks - 1)
        def _():
            st_ref[...] = s_ref[...]


def _ret_sweep(proj, log_gamma, *, tok0, batch, seq, backward, rope_tabs=None, s0=None,
               o_prev=None, gn_g=None, emit_state=False, out_dtype=F32, name):
    C = RET_CHUNK
    d = proj.shape[1] // 6
    dv_all = 2 * d
    n_chunks = seq // C
    blk0 = tok0 // C
    finish = o_prev is not None

    def chunk(c):
        return (n_chunks - 1 - c) if backward else c

    def tok_blk(b, c):
        return blk0 + b * n_chunks + chunk(c)

    def out_blk(b, c):
        return b * n_chunks + chunk(c)

    in_specs = [
        pl.BlockSpec(memory_space=pltpu.SMEM),
        pl.BlockSpec((C, d), lambda b, c: (tok_blk(b, c), 0)),
        pl.BlockSpec((C, d), lambda b, c: (tok_blk(b, c), 1)),
        pl.BlockSpec((C, dv_all), lambda b, c: (tok_blk(b, c), 1)),
    ]
    args = [log_gamma, proj, proj, proj]
    if rope_tabs is not None:
        dk = d // RET_HEADS
        in_specs += [pl.BlockSpec((C, dk), lambda b, c: (chunk(c), 0))] * 2
        args += list(rope_tabs)
    if s0 is not None:
        direction = 1 if backward else 0
        in_specs.append(pl.BlockSpec((None, None) + s0.shape[2:], lambda b, c: (b, direction, 0, 0, 0)))
        args.append(s0)
    if finish:
        in_specs += [
            pl.BlockSpec((C, dv_all), lambda b, c: (out_blk(b, c), 0)),
            pl.BlockSpec((C, dv_all), lambda b, c: (tok_blk(b, c), 2)),
            pl.BlockSpec((1, dv_all), lambda b, c: (0, 0)),
        ]
        args += [o_prev, proj, gn_g.reshape(1, dv_all)]
    state_shape = (RET_HEADS, d // RET_HEADS, dv_all // RET_HEADS)
    out_specs = [pl.BlockSpec((C, dv_all), lambda b, c: (out_blk(b, c), 0))]
    out_shape = [jax.ShapeDtypeStruct((batch * seq, dv_all), out_dtype)]
    if emit_state:
        out_specs.append(pl.BlockSpec((None,) + state_shape, lambda b, c: (b, 0, 0, 0)))
        out_shape.append(jax.ShapeDtypeStruct((batch,) + state_shape, F32))
    res = pl.pallas_call(
        functools.partial(_ret_kernel, backward=backward, rope=rope_tabs is not None,
                          has_s0=s0 is not None, emit_state=emit_state, finish=finish,
                          n_chunks=n_chunks),
        grid=(batch, n_chunks),
        in_specs=in_specs,
        out_specs=out_specs,
        out_shape=out_shape,
        scratch_shapes=[pltpu.VMEM(state_shape, F32)],
        compiler_params=_cparams("parallel", "arbitrary"),
        name=name,
    )(*args)
    return res if emit_state else (res[0], None)


def _rope_tables(seq, dk):
    t = jnp.arange(seq)
    rowp = (t // GRID_W).astype(F32)
    colp = (t % GRID_W).astype(F32)
    nf = dk // 4
    freqs = ROPE_BASE ** (-jnp.arange(nf, dtype=F32) / nf)
    ar = rowp[:, None] * freqs[None, :]
    ac = colp[:, None] * freqs[None, :]
    cos = jnp.concatenate([jnp.cos(ar), jnp.cos(ar), jnp.cos(ac), jnp.cos(ac)], axis=-1)
    sin = jnp.concatenate([-jnp.sin(ar), jnp.sin(ar), -jnp.sin(ac), jnp.sin(ac)], axis=-1)
    return cos, sin


RANK_STEP = 256.0


def _top_values(work, out_ref, h, k, with_rank=False):
    rank = jnp.full(work.shape, RANK_STEP * k, F32) if with_rank else None
    for r in range(k):
        m = jnp.max(work, axis=0, keepdims=True)
        out_ref[r, h:h + 1, :] = m
        hit = work == m
        if with_rank:
            rank = jnp.where(hit, RANK_STEP * r, rank)
        if r + 1 < k:
            work = jnp.where(hit, -jnp.inf, work)
    return rank


def _max_tree(xs):
    xs = list(xs)
    while len(xs) > 1:
        nxt = [jnp.maximum(xs[i], xs[i + 1]) for i in range(0, len(xs) - 1, 2)]
        if len(xs) % 2:
            nxt.append(xs[-1])
        xs = nxt
    return xs[0]


def _peer_sel_kernel(x_ref, g_ref, sh_ref, sc_ref, wqt_ref, k0_ref, k1_ref,
                     xt_ref, cnt_ref, e1_ref, r2_ref, e2_ref,
                     qt_ref, v1_ref, v2_ref, s1_ref, s2_ref, cn_ref):
    K = PEER_TOPK
    half = PEER_DK // 2
    ht = _modulated(x_ref[...], g_ref[...], sh_ref[...], sc_ref[...]).T.astype(BF16)
    xt_ref[...] = ht
    qt_ref[...] = jnp.dot(wqt_ref[...], ht, preferred_element_type=F32)
    for h in range(PEER_HEADS):
        q1 = qt_ref[h * PEER_DK:h * PEER_DK + half, :]
        q2 = qt_ref[h * PEER_DK + half:(h + 1) * PEER_DK, :]
        s1 = jnp.dot(k0_ref[...], q1, precision=HIGHEST, preferred_element_type=F32)
        s2 = jnp.dot(k1_ref[...], q2, precision=HIGHEST, preferred_element_type=F32)
        s1_ref[h] = s1
        s2_ref[h] = s2
        _top_values(s1, v1_ref, h, K)
        r2_ref[h] = _top_values(s2, v2_ref, h, K, with_rank=True).astype(BF16)

    v1 = [v1_ref[r] for r in range(K)]
    v2 = [v2_ref[r] for r in range(K)]
    pairs = [(a, b) for a in range(K) for b in range(K) if (a + 1) * (b + 1) <= K]
    sums = {ab: v1[ab[0]] + v2[ab[1]] for ab in pairs}
    cands = [sums[ab] for ab in pairs]
    top = sums[(0, 0)]
    z = jnp.zeros_like(top)
    mx = top
    for r in range(K):
        mx = _max_tree(cands)
        z = z + jnp.exp(mx - top)
        if r + 1 < K:
            cands = [jnp.where(cd == mx, -jnp.inf, cd) for cd in cands]
    tau = mx
    for a in range(K):
        n = jnp.zeros_like(top)
        for b in range(K):
            if (a, b) in sums:
                n = n + jnp.where(sums[(a, b)] >= tau, RANK_STEP, 0.0)
        cn_ref[a] = n
    inv_z = 1.0 / z
    for h in range(PEER_HEADS):
        s1 = s1_ref[h]
        cnt = jnp.zeros_like(s1)
        for a in range(K):
            cnt = jnp.where(s1 == v1_ref[a, h:h + 1, :], cn_ref[a, h:h + 1, :], cnt)
        cnt_ref[h] = cnt
        e1_ref[h] = jnp.exp(s1 - v1_ref[0, h:h + 1, :])
        e2_ref[h] = (jnp.exp(s2_ref[h] - v2_ref[0, h:h + 1, :]) * inv_z[h:h + 1, :]).astype(BF16)


def _peer_select(x, g, mod3, wq_t, keys, *, t_prompt, dec_seq, tt=256):
    t, d = x.shape
    fac_spec = pl.BlockSpec((PEER_HEADS, N_KEYS, tt), lambda i: (0, 0, i))
    fac32 = jax.ShapeDtypeStruct((PEER_HEADS, N_KEYS, t), F32)
    fac16 = jax.ShapeDtypeStruct((PEER_HEADS, N_KEYS, t), BF16)
    return pl.pallas_call(
        _peer_sel_kernel,
        grid=(t // tt,),
        in_specs=[
            pl.BlockSpec((tt, d), lambda i: (i, 0)),
            pl.BlockSpec((1, d), lambda i: (0, 0)),
            _mod_spec(3, tt, t_prompt, dec_seq, d),
            _mod_spec(4, tt, t_prompt, dec_seq, d),
            pl.BlockSpec((PEER_HEADS * PEER_DK, d), lambda i: (0, 0)),
            pl.BlockSpec((N_KEYS, PEER_DK // 2), lambda i: (0, 0)),
            pl.BlockSpec((N_KEYS, PEER_DK // 2), lambda i: (0, 0)),
        ],
        out_specs=[pl.BlockSpec((d, tt), lambda i: (0, i)), fac_spec, fac_spec, fac_spec, fac_spec],
        out_shape=[jax.ShapeDtypeStruct((d, t), BF16), fac32, fac32, fac16, fac16],
        scratch_shapes=[
            pltpu.VMEM((PEER_HEADS * PEER_DK, tt), F32),
            pltpu.VMEM((PEER_TOPK, PEER_HEADS, tt), F32),
            pltpu.VMEM((PEER_TOPK, PEER_HEADS, tt), F32),
            pltpu.VMEM((PEER_HEADS, N_KEYS, tt), F32),
            pltpu.VMEM((PEER_HEADS, N_KEYS, tt), F32),
            pltpu.VMEM((PEER_TOPK, PEER_HEADS, tt), F32),
        ],
        compiler_params=_cparams("parallel"),
        name="peer_select",
    )(x, g.reshape(1, d), mod3, mod3, wq_t, keys[0], keys[1])


PEER_GATE_ROWS = 16


def _peer_dense_kernel(xt_ref, u_ref, vt_ref, cntg_ref, e1g_ref, r2_ref, e2_ref,
                       x_ref, gate_ref, fg_ref, o_ref, acc_ref, a_ref, w_ref, cntb_ref, e1b_ref, zero_ref, *,
                       n_tiles, final_norm):
    j = pl.program_id(1)
    te, tt = a_ref.shape
    ng = te // N_KEYS
    rb = PEER_GATE_ROWS

    @pl.when(j == 0)
    def _():
        acc_ref[...] = jnp.zeros_like(acc_ref)

    a_ref[...] = jnp.dot(u_ref[...], xt_ref[...], preferred_element_type=F32)

    zero_ref[...] = jnp.zeros_like(zero_ref)
    for h in range(PEER_HEADS):
        for g in range(ng):
            cntb_ref[h, g] = jnp.broadcast_to(cntg_ref[h, g:g + 1, :], (rb, tt)).astype(BF16)
            e1b_ref[h, g] = jnp.broadcast_to(e1g_ref[h, g:g + 1, :], (rb, tt)).astype(BF16)

    for g in range(ng):
        for c in range(tt // 128):
            ls = slice(c * 128, (c + 1) * 128)
            zero = zero_ref[...]
            gsum = [zero for _ in range(N_KEYS // rb)]
            for h in range(PEER_HEADS):
                cnt = cntb_ref[h, g, :, ls]
                e1 = e1b_ref[h, g, :, ls]
                for r in range(N_KEYS // rb):
                    i2 = slice(r * rb, (r + 1) * rb)
                    room = cnt - r2_ref[h, i2, ls]
                    keep = jnp.maximum(jnp.minimum(room, e2_ref[h, i2, ls]), zero)
                    gsum[r] = gsum[r] + e1 * keep
            for r in range(N_KEYS // rb):
                rows = slice(g * N_KEYS + r * rb, g * N_KEYS + (r + 1) * rb)
                w_ref[rows, ls] = gsum[r] * jax.nn.gelu(a_ref[rows, ls]).astype(BF16)

    acc_ref[...] += jnp.dot(vt_ref[...], w_ref[...], preferred_element_type=F32)

    @pl.when(j == n_tiles - 1)
    def _():
        y = x_ref[...] + gate_ref[...] * acc_ref[...].T
        if final_norm:
            ms = jnp.mean(y * y, axis=-1, keepdims=True)
            y = y * lax.rsqrt(ms + EPS) * fg_ref[...]
        o_ref[...] = y


def _peer_dense(xt, u, vt, cnt, e1, r2, e2, x, mod3, final_g, *, final_norm,
                t_prompt, dec_seq, tt=512, te=1024):
    t, d = x.shape
    n_exp = u.shape[0]
    n_tiles = n_exp // te
    ng = te // N_KEYS
    once = pl.Buffered(1)
    fac_spec = pl.BlockSpec((PEER_HEADS, N_KEYS, tt), lambda i, j: (0, 0, i), pipeline_mode=once)
    row_spec = pl.BlockSpec((PEER_HEADS, ng, tt), lambda i, j: (0, j, i))
    return pl.pallas_call(
        functools.partial(_peer_dense_kernel, n_tiles=n_tiles, final_norm=final_norm),
        grid=(t // tt, n_tiles),
        in_specs=[
            pl.BlockSpec((d, tt), lambda i, j: (0, i), pipeline_mode=once),
            pl.BlockSpec((te, d), lambda i, j: (j, 0)),
            pl.BlockSpec((d, te), lambda i, j: (0, j)),
            row_spec, row_spec, fac_spec, fac_spec,
            pl.BlockSpec((tt, d), lambda i, j: (i, 0), pipeline_mode=once),
            _mod_spec(5, tt, t_prompt, dec_seq, d),
            pl.BlockSpec((1, d), lambda i, j: (0, 0)),
        ],
        out_specs=pl.BlockSpec((tt, d), lambda i, j: (i, 0)),
        out_shape=jax.ShapeDtypeStruct((t, d), F32),
        scratch_shapes=[
            pltpu.VMEM((d, tt), F32),
            pltpu.VMEM((te, tt), F32),
            pltpu.VMEM((te, tt), BF16),
            pltpu.VMEM((PEER_HEADS, ng, PEER_GATE_ROWS, tt), BF16),
            pltpu.VMEM((PEER_HEADS, ng, PEER_GATE_ROWS, tt), BF16),
            pltpu.VMEM((PEER_GATE_ROWS, 128), BF16),
        ],
        compiler_params=_cparams("parallel", "arbitrary"),
        name="peer_dense",
    )(xt, u, vt, cnt, e1, r2, e2, x, mod3, final_g.reshape(1, d))


def _peer_layer(x, norm_g, mod3, w_q, keys, u, v, final_g, *, final_norm, t_prompt, dec_seq):
    wq_t = w_q.T.astype(BF16)
    xt, cnt, e1, r2, e2 = _peer_select(x, norm_g, mod3, wq_t, keys, t_prompt=t_prompt, dec_seq=dec_seq)
    return _peer_dense(xt, u.astype(BF16), v.T.astype(BF16), cnt, e1, r2, e2, x, mod3, final_g,
                       final_norm=final_norm, t_prompt=t_prompt, dec_seq=dec_seq)


def _glu(val, gate):
    return val * jax.nn.sigmoid(gate)


def _identity(y):
    return y


def kernel(x_prompt, x_sample, state_ret, c, c_ctx, w_mod, b_mod, norm_mix_g, norm_ffn_g, conv_w_in, conv_b_in, conv_dw, conv_dw_b, conv_norm_g, conv_w_out, conv_b_out, ret_w_in, ret_decay, ret_gn_g, ret_w_out, peer_w_q, peer_keys, peer_u, peer_v, final_norm_g):
    batch, seq, d = x_prompt.shape
    dec_batch, dec_seq, _ = x_sample.shape
    depth = w_mod.shape[0]
    t_prompt = batch * seq
    t_sample = dec_batch * dec_seq
    sizes = dict(t_prompt=t_prompt, dec_seq=dec_seq)

    x = jnp.concatenate([x_prompt.reshape(t_prompt, d), x_sample.reshape(t_sample, d)], axis=0)
    cond = jnp.zeros((COND_ROWS, d), F32).at[0].set(c_ctx).at[1:1 + dec_batch].set(c)
    mod = _ada_table(cond, w_mod, b_mod)

    new_states = []
    for l in range(depth):
        mod3 = mod[l].reshape(COND_ROWS, 1, 6 * d)
        i = l // 2
        if l % 2 == 0:
            a = _modmm(x, norm_mix_g[l], mod3, (0, 1), conv_w_in[i].astype(BF16), (0, 1), d, F32, _glu,
                       bias=conv_b_in[i], name="conv_in", **sizes)
            cv = _dwconv(a, conv_dw[i], conv_dw_b[i], conv_norm_g[i], t_prompt=t_prompt, seq=seq)
            x = _mm_residual(cv, conv_w_out[i].astype(BF16), conv_b_out[i], x, mod3, 2,
                             name="conv_out", **sizes)
        else:
            proj = _modmm(x, norm_mix_g[l], mod3, (0, 1), ret_w_in[i].astype(BF16), (0,), 6 * d, BF16,
                          _identity, name="ret_in", **sizes)
            log_gamma = jax.nn.log_sigmoid(ret_decay[i].astype(F32))
            of, sf = _ret_sweep(proj, log_gamma, tok0=0, batch=batch, seq=seq, backward=False,
                                emit_state=True, name="ret_ctx_fwd")
            yp, sb = _ret_sweep(proj, log_gamma, tok0=0, batch=batch, seq=seq, backward=True,
                                o_prev=of, gn_g=ret_gn_g[i], emit_state=True, out_dtype=BF16,
                                name="ret_ctx_bwd")
            new_states.append(jnp.stack([sf, sb], axis=1))
            tabs = _rope_tables(dec_seq, d // RET_HEADS)
            s0 = state_ret[:, i]
            of, _ = _ret_sweep(proj, log_gamma, tok0=t_prompt, batch=dec_batch, seq=dec_seq,
                               backward=False, rope_tabs=tabs, s0=s0, name="ret_lat_fwd")
            ys, _ = _ret_sweep(proj, log_gamma, tok0=t_prompt, batch=dec_batch, seq=dec_seq,
                               backward=True, rope_tabs=tabs, s0=s0, o_prev=of, gn_g=ret_gn_g[i],
                               out_dtype=BF16, name="ret_lat_bwd")
            y = jnp.concatenate([yp, ys], axis=0)
            x = _mm_residual(y, ret_w_out[i].astype(BF16), jnp.zeros((d,), F32), x, mod3, 2,
                             name="ret_out", **sizes)
        x = _peer_layer(x, norm_ffn_g[l], mod3, peer_w_q[l], peer_keys[l], peer_u[l], peer_v[l],
                        final_norm_g, final_norm=(l == depth - 1), **sizes)

    y_prompt = x[:t_prompt].reshape(batch, seq, d)
    y_sample = x[t_prompt:].reshape(dec_batch, dec_seq, d)
    return (y_prompt, y_sample, jnp.stack(new_states, axis=1))
```

```python
import functools
import math

import jax
import jax.numpy as jnp
from jax import lax
from jax.experimental import pallas as pl
from jax.experimental.pallas import tpu as pltpu

F32 = jnp.float32
BF16 = jnp.bfloat16
HIGHEST = lax.Precision.HIGHEST

EPS = 1e-6
GRID_W = 64
CONV_W = 31
RET_HEADS = 8
RET_CHUNK = 256
ROPE_BASE = 10000.0
PEER_HEADS = 8
PEER_DK = 256
N_KEYS = 128
PEER_TOPK = 16
COND_ROWS = 8
VMEM_LIMIT = 56 * 1024 * 1024


def _cparams(*sem):
    return pltpu.CompilerParams(dimension_semantics=sem, vmem_limit_bytes=VMEM_LIMIT)


def _cond_row(i, tm, t_prompt, dec_seq):
    t0 = i * tm
    return jnp.where(t0 < t_prompt, 0, 1 + (t0 - t_prompt) // dec_seq)


def _mod_spec(chunk, tm, t_prompt, dec_seq, width, ncol=1):
    nblk = (ncol,)

    def idx(i, *rest):
        j = rest[0] if (rest and ncol > 1) else 0
        return (_cond_row(i, tm, t_prompt, dec_seq), 0, chunk * nblk[0] + j)

    return pl.BlockSpec((None, 1, width), idx)


def _ada_kernel(cond_ref, w_ref, b_ref, o_ref):
    c = cond_ref[...]
    s = c * jax.nn.sigmoid(c)
    o_ref[...] = jnp.dot(s, w_ref[...], precision=HIGHEST, preferred_element_type=F32) + b_ref[...]


def _ada_table(cond, w_mod, b_mod):
    depth, d, n = w_mod.shape
    tn = 1024
    return pl.pallas_call(
        _ada_kernel,
        grid=(depth, n // tn),
        in_specs=[
            pl.BlockSpec((COND_ROWS, d), lambda l, j: (0, 0)),
            pl.BlockSpec((None, d, tn), lambda l, j: (l, 0, j)),
            pl.BlockSpec((None, 1, tn), lambda l, j: (l, 0, j)),
        ],
        out_specs=pl.BlockSpec((None, COND_ROWS, tn), lambda l, j: (l, 0, j)),
        out_shape=jax.ShapeDtypeStruct((depth, COND_ROWS, n), F32),
        compiler_params=_cparams("parallel", "parallel"),
        name="ada_table",
    )(cond, w_mod, b_mod.reshape(depth, 1, n))


def _modulated(x, g, sh, sc):
    ms = jnp.mean(x * x, axis=-1, keepdims=True)
    y = x * lax.rsqrt(ms + EPS) * g
    return y * (1.0 + sc) + sh


def _modmm_kernel(*refs, n_w, has_bias, epilogue):
    x_ref, g_ref, sh_ref, sc_ref = refs[:4]
    w_refs = refs[4:4 + n_w]
    b_refs = refs[4 + n_w:4 + n_w * (2 if has_bias else 1)]
    o_ref, h_ref = refs[-2], refs[-1]

    @pl.when(pl.program_id(1) == 0)
    def _():
        h_ref[...] = _modulated(x_ref[...], g_ref[...], sh_ref[...], sc_ref[...]).astype(BF16)

    h = h_ref[...]
    accs = [jnp.dot(h, w[...], preferred_element_type=F32) for w in w_refs]
    if has_bias:
        accs = [a + b[...] for a, b in zip(accs, b_refs)]
    o_ref[...] = epilogue(*accs).astype(o_ref.dtype)


def _modmm(x, g, mod3, chunks, w, w_col_blocks, n_out, out_dtype, epilogue, bias=None, *,
           t_prompt, dec_seq, tm=512, tn=1024, name):
    t, d = x.shape
    nj = n_out // tn
    n_w = len(w_col_blocks)
    in_specs = [
        pl.BlockSpec((tm, d), lambda i, j: (i, 0)),
        pl.BlockSpec((1, d), lambda i, j: (0, 0)),
        _mod_spec(chunks[0], tm, t_prompt, dec_seq, d),
        _mod_spec(chunks[1], tm, t_prompt, dec_seq, d),
    ]
    args = [x, g.reshape(1, d), mod3, mod3]
    for c in w_col_blocks:
        in_specs.append(pl.BlockSpec((d, tn), lambda i, j, c=c: (0, c * nj + j)))
        args.append(w)
    if bias is not None:
        b2 = bias.reshape(1, -1)
        for c in w_col_blocks:
            in_specs.append(pl.BlockSpec((1, tn), lambda i, j, c=c: (0, c * nj + j)))
            args.append(b2)
    return pl.pallas_call(
        functools.partial(_modmm_kernel, n_w=n_w, has_bias=bias is not None, epilogue=epilogue),
        grid=(t // tm, nj),
        in_specs=in_specs,
        out_specs=pl.BlockSpec((tm, tn), lambda i, j: (i, j)),
        out_shape=jax.ShapeDtypeStruct((t, n_out), out_dtype),
        scratch_shapes=[pltpu.VMEM((tm, d), BF16)],
        compiler_params=_cparams("parallel", "arbitrary"),
        name=name,
    )(*args)


def _mmres_kernel(a_ref, w_ref, b_ref, x_ref, gate_ref, o_ref):
    y = jnp.dot(a_ref[...], w_ref[...], preferred_element_type=F32) + b_ref[...]
    o_ref[...] = x_ref[...] + gate_ref[...] * y


def _mm_residual(a, w, bias, x, mod3, gate_chunk, *, t_prompt, dec_seq, tm=512, tn=1024, name):
    t, k = a.shape
    d = x.shape[1]
    nj = d // tn
    return pl.pallas_call(
        _mmres_kernel,
        grid=(t // tm, nj),
        in_specs=[
            pl.BlockSpec((tm, k), lambda i, j: (i, 0)),
            pl.BlockSpec((k, tn), lambda i, j: (0, j)),
            pl.BlockSpec((1, tn), lambda i, j: (0, j)),
            pl.BlockSpec((tm, tn), lambda i, j: (i, j)),
            _mod_spec(gate_chunk, tm, t_prompt, dec_seq, tn, ncol=nj),
        ],
        out_specs=pl.BlockSpec((tm, tn), lambda i, j: (i, j)),
        out_shape=jax.ShapeDtypeStruct((t, d), F32),
        compiler_params=_cparams("parallel", "arbitrary"),
        name=name,
    )(a, w, bias.reshape(1, d), x, mod3)


CONV_SEG = GRID_W
CONV_HALO = 16
CONV_BLOCK = 256


def _dwconv_kernel(a_ref, dw_ref, dwb_ref, ng_ref, o_ref, buf_ref, c_ref, win_ref, *, joined_blocks):
    nseg = CONV_BLOCK // CONV_SEG
    d = a_ref.shape[1]
    joined = (pl.program_id(0) < joined_blocks).astype(F32)
    zeros = jnp.zeros((CONV_HALO, d), F32)
    for s in range(nseg):
        lo = s * CONV_SEG
        buf_ref[s, CONV_HALO:CONV_HALO + CONV_SEG, :] = a_ref[lo:lo + CONV_SEG, :]
        if s == 0:
            buf_ref[s, 0:CONV_HALO, :] = zeros
        else:
            buf_ref[s, 0:CONV_HALO, :] = a_ref[lo - CONV_HALO:lo, :] * joined
        if s == nseg - 1:
            buf_ref[s, CONV_HALO + CONV_SEG:, :] = zeros
        else:
            buf_ref[s, CONV_HALO + CONV_SEG:, :] = a_ref[lo + CONV_SEG:lo + CONV_SEG + CONV_HALO, :] * joined

    lane = 256
    sub = 8
    base = CONV_HALO - CONV_W // 2
    span = CONV_SEG + 2 * CONV_HALO - sub
    for s in range(nseg):
        for c in range(d // lane):
            ls = slice(c * lane, (c + 1) * lane)
            acc = jnp.zeros((CONV_SEG, lane), F32) + dwb_ref[:, ls]
            for r in range(sub):
                win_ref[...] = buf_ref[s, r:r + span, ls]
                for k in range(CONV_W):
                    if (base + k) % sub == r:
                        q = (base + k) // sub * sub
                        acc = acc + win_ref[q:q + CONV_SEG, :] * dw_ref[k:k + 1, ls]
            c_ref[s * CONV_SEG:(s + 1) * CONV_SEG, ls] = acc

    y = c_ref[...]
    ms = jnp.mean(y * y, axis=-1, keepdims=True)
    y = y * lax.rsqrt(ms + EPS) * ng_ref[...]
    o_ref[...] = (y * jax.nn.sigmoid(y)).astype(o_ref.dtype)


def _dwconv(a, dw, dw_b, norm_g, *, t_prompt, seq):
    t, d = a.shape
    assert seq == CONV_BLOCK
    nseg = CONV_BLOCK // CONV_SEG
    return pl.pallas_call(
        functools.partial(_dwconv_kernel, joined_blocks=t_prompt // CONV_BLOCK),
        grid=(t // CONV_BLOCK,),
        in_specs=[
            pl.BlockSpec((CONV_BLOCK, d), lambda i: (i, 0)),
            pl.BlockSpec((CONV_W, d), lambda i: (0, 0)),
            pl.BlockSpec((1, d), lambda i: (0, 0)),
            pl.BlockSpec((1, d), lambda i: (0, 0)),
        ],
        out_specs=pl.BlockSpec((CONV_BLOCK, d), lambda i: (i, 0)),
        out_shape=jax.ShapeDtypeStruct((t, d), BF16),
        scratch_shapes=[
            pltpu.VMEM((nseg, CONV_SEG + 2 * CONV_HALO, d), F32),
            pltpu.VMEM((CONV_BLOCK, d), F32),
            pltpu.VMEM((CONV_SEG + 2 * CONV_HALO - 8, 256), F32),
        ],
        compiler_params=_cparams("parallel"),
        name="dwconv",
    )(a, dw, dw_b.reshape(1, d), norm_g.reshape(1, d))


def _swap_halves(x):
    parts = [pltpu.roll(x[:, g * 128:(g + 1) * 128], 64, axis=1) for g in range(x.shape[1] // 128)]
    return jnp.concatenate(parts, axis=1) if len(parts) > 1 else parts[0]


def _ret_kernel(*refs, backward, rope, has_s0, emit_state, finish, n_chunks):
    it = iter(refs)
    lg_ref, q_ref, k_ref, v_ref = next(it), next(it), next(it), next(it)
    cos_ref, sin_ref = (next(it), next(it)) if rope else (None, None)
    s0_ref = next(it) if has_s0 else None
    op_ref, gate_ref, gn_ref = (next(it), next(it), next(it)) if finish else (None, None, None)
    o_ref = next(it)
    st_ref = next(it) if emit_state else None
    s_ref = next(it)

    c = pl.program_id(1)
    C = q_ref.shape[0]
    dk = q_ref.shape[1] // RET_HEADS
    dv = v_ref.shape[1] // RET_HEADS
    direction = 1 if backward else 0

    @pl.when(c == 0)
    def _():
        if has_s0:
            s_ref[...] = s0_ref[...]
        else:
            s_ref[...] = jnp.zeros_like(s_ref)

    row = lax.broadcasted_iota(jnp.int32, (C, C), 0)
    col = lax.broadcasted_iota(jnp.int32, (C, C), 1)
    rel = (col - row) if backward else (row - col)
    relf = jnp.maximum(rel, 0).astype(F32)
    pos = lax.broadcasted_iota(jnp.int32, (C, dk), 0)
    xi_e = ((C - pos) if backward else (pos + 1)).astype(F32)
    zeta_e = (pos if backward else (C - 1 - pos)).astype(F32)

    for h in range(RET_HEADS):
        lg = lg_ref[direction, h]
        q = q_ref[:, h * dk:(h + 1) * dk].astype(F32)
        k = k_ref[:, h * dk:(h + 1) * dk].astype(F32) * (dk ** -0.5)
        if rope:
            cs, sn = cos_ref[...], sin_ref[...]
            q = q * cs + _swap_halves(q) * sn
            k = k * cs + _swap_halves(k) * sn
        v = v_ref[:, h * dv:(h + 1) * dv]
        decay = jnp.where(rel >= 0, jnp.exp(lg * relf), 0.0)
        scores = lax.dot_general(q.astype(BF16), k.astype(BF16), (((1,), (1,)), ((), ())),
                                 preferred_element_type=F32) * decay
        inner = jnp.dot(scores.astype(BF16), v, preferred_element_type=F32)
        s_old = s_ref[h]
        cross = jnp.dot((q * jnp.exp(lg * xi_e)).astype(BF16), s_old.astype(BF16),
                        preferred_element_type=F32)
        kz_t = (k * jnp.exp(lg * zeta_e)).T.astype(BF16)
        s_new = jnp.exp(lg * C) * s_old + jnp.dot(kz_t, v, preferred_element_type=F32)
        s_ref[h] = s_new
        o = inner + cross
        if finish:
            o = o + op_ref[:, h * dv:(h + 1) * dv]
            mu = jnp.mean(o, axis=-1, keepdims=True)
            oc = o - mu
            var = jnp.mean(oc * oc, axis=-1, keepdims=True)
            o = oc * lax.rsqrt(var + EPS) * gn_ref[:, h * dv:(h + 1) * dv]
            g = gate_ref[:, h * dv:(h + 1) * dv].astype(F32)
            o = g * jax.nn.sigmoid(g) * o
        o_ref[:, h * dv:(h + 1) * dv] = o.astype(o_ref.dtype)

    if emit_state:
        @pl.when(c == n_chunks - 1)
        def _():
            st_ref[...] = s_ref[...]


def _ret_sweep(proj, log_gamma, *, tok0, batch, seq, backward, rope_tabs=None, s0=None,
               o_prev=None, gn_g=None, emit_state=False, out_dtype=F32, name):
    C = RET_CHUNK
    d = proj.shape[1] // 6
    dv_all = 2 * d
    n_chunks = seq // C
    blk0 = tok0 // C
    finish = o_prev is not None

    def chunk(c):
        return (n_chunks - 1 - c) if backward else c

    def tok_blk(b, c):
        return blk0 + b * n_chunks + chunk(c)

    def out_blk(b, c):
        return b * n_chunks + chunk(c)

    in_specs = [
        pl.BlockSpec(memory_space=pltpu.SMEM),
        pl.BlockSpec((C, d), lambda b, c: (tok_blk(b, c), 0)),
        pl.BlockSpec((C, d), lambda b, c: (tok_blk(b, c), 1)),
        pl.BlockSpec((C, dv_all), lambda b, c: (tok_blk(b, c), 1)),
    ]
    args = [log_gamma, proj, proj, proj]
    if rope_tabs is not None:
        dk = d // RET_HEADS
        in_specs += [pl.BlockSpec((C, dk), lambda b, c: (chunk(c), 0))] * 2
        args += list(rope_tabs)
    if s0 is not None:
        direction = 1 if backward else 0
        in_specs.append(pl.BlockSpec((None, None) + s0.shape[2:], lambda b, c: (b, direction, 0, 0, 0)))
        args.append(s0)
    if finish:
        in_specs += [
            pl.BlockSpec((C, dv_all), lambda b, c: (out_blk(b, c), 0)),
            pl.BlockSpec((C, dv_all), lambda b, c: (tok_blk(b, c), 2)),
            pl.BlockSpec((1, dv_all), lambda b, c: (0, 0)),
        ]
        args += [o_prev, proj, gn_g.reshape(1, dv_all)]
    state_shape = (RET_HEADS, d // RET_HEADS, dv_all // RET_HEADS)
    out_specs = [pl.BlockSpec((C, dv_all), lambda b, c: (out_blk(b, c), 0))]
    out_shape = [jax.ShapeDtypeStruct((batch * seq, dv_all), out_dtype)]
    if emit_state:
        out_specs.append(pl.BlockSpec((None,) + state_shape, lambda b, c: (b, 0, 0, 0)))
        out_shape.append(jax.ShapeDtypeStruct((batch,) + state_shape, F32))
    res = pl.pallas_call(
        functools.partial(_ret_kernel, backward=backward, rope=rope_tabs is not None,
                          has_s0=s0 is not None, emit_state=emit_state, finish=finish,
                          n_chunks=n_chunks),
        grid=(batch, n_chunks),
        in_specs=in_specs,
        out_specs=out_specs,
        out_shape=out_shape,
        scratch_shapes=[pltpu.VMEM(state_shape, F32)],
        compiler_params=_cparams("parallel", "arbitrary"),
        name=name,
    )(*args)
    return res if emit_state else (res[0], None)


def _rope_tables(seq, dk):
    t = jnp.arange(seq)
    rowp = (t // GRID_W).astype(F32)
    colp = (t % GRID_W).astype(F32)
    nf = dk // 4
    freqs = ROPE_BASE ** (-jnp.arange(nf, dtype=F32) / nf)
    ar = rowp[:, None] * freqs[None, :]
    ac = colp[:, None] * freqs[None, :]
    cos = jnp.concatenate([jnp.cos(ar), jnp.cos(ar), jnp.cos(ac), jnp.cos(ac)], axis=-1)
    sin = jnp.concatenate([-jnp.sin(ar), jnp.sin(ar), -jnp.sin(ac), jnp.sin(ac)], axis=-1)
    return cos, sin


RANK_STEP = 256.0


def _top_values(work, out_ref, h, k, with_rank=False):
    rank = jnp.full(work.shape, RANK_STEP * k, F32) if with_rank else None
    for r in range(k):
        m = jnp.max(work, axis=0, keepdims=True)
        out_ref[r, h:h + 1, :] = m
        hit = work == m
        if with_rank:
            rank = jnp.where(hit, RANK_STEP * r, rank)
        if r + 1 < k:
            work = jnp.where(hit, -jnp.inf, work)
    return rank


def _max_tree(xs):
    xs = list(xs)
    while len(xs) > 1:
        nxt = [jnp.maximum(xs[i], xs[i + 1]) for i in range(0, len(xs) - 1, 2)]
        if len(xs) % 2:
            nxt.append(xs[-1])
        xs = nxt
    return xs[0]


def _peer_sel_kernel(x_ref, g_ref, sh_ref, sc_ref, wqt_ref, k0_ref, k1_ref,
                     xt_ref, cnt_ref, e1_ref, r2_ref, e2_ref,
                     qt_ref, v1_ref, v2_ref, s1_ref, s2_ref, cn_ref):
    K = PEER_TOPK
    half = PEER_DK // 2
    ht = _modulated(x_ref[...], g_ref[...], sh_ref[...], sc_ref[...]).T.astype(BF16)
    xt_ref[...] = ht
    qt_ref[...] = jnp.dot(wqt_ref[...], ht, preferred_element_type=F32)
    for h in range(PEER_HEADS):
        q1 = qt_ref[h * PEER_DK:h * PEER_DK + half, :]
        q2 = qt_ref[h * PEER_DK + half:(h + 1) * PEER_DK, :]
        s1 = jnp.dot(k0_ref[...], q1, precision=HIGHEST, preferred_element_type=F32)
        s2 = jnp.dot(k1_ref[...], q2, precision=HIGHEST, preferred_element_type=F32)
        s1_ref[h] = s1
        s2_ref[h] = s2
        _top_values(s1, v1_ref, h, K)
        r2_ref[h] = _top_values(s2, v2_ref, h, K, with_rank=True).astype(BF16)

    v1 = [v1_ref[r] for r in range(K)]
    v2 = [v2_ref[r] for r in range(K)]
    pairs = [(a, b) for a in range(K) for b in range(K) if (a + 1) * (b + 1) <= K]
    sums = {ab: v1[ab[0]] + v2[ab[1]] for ab in pairs}
    cands = [sums[ab] for ab in pairs]
    top = sums[(0, 0)]
    z = jnp.zeros_like(top)
    mx = top
    for r in range(K):
        mx = _max_tree(cands)
        z = z + jnp.exp(mx - top)
        if r + 1 < K:
            cands = [jnp.where(cd == mx, -jnp.inf, cd) for cd in cands]
    tau = mx
    for a in range(K):
        n = jnp.zeros_like(top)
        for b in range(K):
            if (a, b) in sums:
                n = n + jnp.where(sums[(a, b)] >= tau, RANK_STEP, 0.0)
        cn_ref[a] = n
    inv_z = 1.0 / z
    for h in range(PEER_HEADS):
        s1 = s1_ref[h]
        cnt = jnp.zeros_like(s1)
        for a in range(K):
            cnt = jnp.where(s1 == v1_ref[a, h:h + 1, :], cn_ref[a, h:h + 1, :], cnt)
        cnt_ref[h] = cnt
        e1_ref[h] = jnp.exp(s1 - v1_ref[0, h:h + 1, :])
        e2_ref[h] = (jnp.exp(s2_ref[h] - v2_ref[0, h:h + 1, :]) * inv_z[h:h + 1, :]).astype(BF16)


def _peer_select(x, g, mod3, wq_t, keys, *, t_prompt, dec_seq, tt=256):
    t, d = x.shape
    fac_spec = pl.BlockSpec((PEER_HEADS, N_KEYS, tt), lambda i: (0, 0, i))
    fac32 = jax.ShapeDtypeStruct((PEER_HEADS, N_KEYS, t), F32)
    fac16 = jax.ShapeDtypeStruct((PEER_HEADS, N_KEYS, t), BF16)
    return pl.pallas_call(
        _peer_sel_kernel,
        grid=(t // tt,),
        in_specs=[
            pl.BlockSpec((tt, d), lambda i: (i, 0)),
            pl.BlockSpec((1, d), lambda i: (0, 0)),
            _mod_spec(3, tt, t_prompt, dec_seq, d),
            _mod_spec(4, tt, t_prompt, dec_seq, d),
            pl.BlockSpec((PEER_HEADS * PEER_DK, d), lambda i: (0, 0)),
            pl.BlockSpec((N_KEYS, PEER_DK // 2), lambda i: (0, 0)),
            pl.BlockSpec((N_KEYS, PEER_DK // 2), lambda i: (0, 0)),
        ],
        out_specs=[pl.BlockSpec((d, tt), lambda i: (0, i)), fac_spec, fac_spec, fac_spec, fac_spec],
        out_shape=[jax.ShapeDtypeStruct((d, t), BF16), fac32, fac32, fac16, fac16],
        scratch_shapes=[
            pltpu.VMEM((PEER_HEADS * PEER_DK, tt), F32),
            pltpu.VMEM((PEER_TOPK, PEER_HEADS, tt), F32),
            pltpu.VMEM((PEER_TOPK, PEER_HEADS, tt), F32),
            pltpu.VMEM((PEER_HEADS, N_KEYS, tt), F32),
            pltpu.VMEM((PEER_HEADS, N_KEYS, tt), F32),
            pltpu.VMEM((PEER_TOPK, PEER_HEADS, tt), F32),
        ],
        compiler_params=_cparams("parallel"),
        name="peer_select",
    )(x, g.reshape(1, d), mod3, mod3, wq_t, keys[0], keys[1])


PEER_GATE_ROWS = 16
PEER_SPLIT = 2


def _peer_dense_kernel(xt_ref, u_ref, vt_ref, cntg_ref, e1g_ref, r2_ref, e2_ref,
                       x_ref, gate_ref, fg_ref, *rest, n_tiles, final_norm, n_split_blocks):
    n_out = 1 if n_split_blocks is None else 2
    o_ref = rest[0]
    o2_ref = rest[1] if n_out == 2 else None
    acc_ref, a_ref, w_ref, cntb_ref, e1b_ref, zero_ref = rest[n_out:]
    j = pl.program_id(1)
    te, tt = a_ref.shape
    ng = te // N_KEYS
    rb = PEER_GATE_ROWS

    @pl.when(j == 0)
    def _():
        acc_ref[...] = jnp.zeros_like(acc_ref)

    gper = ng // PEER_SPLIT
    slabs = [slice(k * gper * N_KEYS, (k + 1) * gper * N_KEYS) for k in range(PEER_SPLIT)]
    for ks in slabs:
        a_ref[ks, :] = jnp.dot(u_ref[ks, :], xt_ref[...], preferred_element_type=F32)

    zero_ref[...] = jnp.zeros_like(zero_ref)
    for h in range(PEER_HEADS):
        for g in range(ng):
            cntb_ref[h, g] = jnp.broadcast_to(cntg_ref[h, g:g + 1, :], (rb, tt)).astype(BF16)
            e1b_ref[h, g] = jnp.broadcast_to(e1g_ref[h, g:g + 1, :], (rb, tt)).astype(BF16)

    for k, ks in enumerate(slabs):
        for g in range(k * gper, (k + 1) * gper):
            for c in range(tt // 128):
                ls = slice(c * 128, (c + 1) * 128)
                zero = zero_ref[...]
                gsum = [zero for _ in range(N_KEYS // rb)]
                for h in range(PEER_HEADS):
                    cnt = cntb_ref[h, g, :, ls]
                    e1 = e1b_ref[h, g, :, ls]
                    for r in range(N_KEYS // rb):
                        i2 = slice(r * rb, (r + 1) * rb)
                        room = cnt - r2_ref[h, i2, ls]
                        keep = jnp.maximum(jnp.minimum(room, e2_ref[h, i2, ls]), zero)
                        gsum[r] = gsum[r] + e1 * keep
                for r in range(N_KEYS // rb):
                    rows = slice(g * N_KEYS + r * rb, g * N_KEYS + (r + 1) * rb)
                    w_ref[rows, ls] = gsum[r] * jax.nn.gelu(a_ref[rows, ls]).astype(BF16)
        acc_ref[...] += jnp.dot(vt_ref[:, ks], w_ref[ks, :], preferred_element_type=F32)

    @pl.when(j == n_tiles - 1)
    def _():
        y = x_ref[...] + gate_ref[...] * acc_ref[...].T
        if final_norm:
            ms = jnp.mean(y * y, axis=-1, keepdims=True)
            y = y * lax.rsqrt(ms + EPS) * fg_ref[...]
        if n_split_blocks is None:
            o_ref[...] = y
        else:
            @pl.when(pl.program_id(0) < n_split_blocks)
            def _():
                o_ref[...] = y

            @pl.when(pl.program_id(0) >= n_split_blocks)
            def _():
                o2_ref[...] = y


PEER_TE = 1024


def _peer_dense(xt, u, vt_tiles, cnt, e1, r2, e2, x, mod3, final_g, *, final_norm, split_outputs,
                t_prompt, dec_seq, tt=512):
    t, d = x.shape
    n_tiles, _, te = vt_tiles.shape
    ng = te // N_KEYS
    once = pl.Buffered(1)
    fac_spec = pl.BlockSpec((PEER_HEADS, N_KEYS, tt), lambda i, j: (0, 0, i), pipeline_mode=once)
    row_spec = pl.BlockSpec((PEER_HEADS, ng, tt), lambda i, j: (0, j, i))
    if split_outputs:
        nb = t_prompt // tt
        out_specs = [pl.BlockSpec((tt, d), lambda i, j: (jnp.minimum(i, nb - 1), 0)),
                     pl.BlockSpec((tt, d), lambda i, j: (jnp.maximum(i - nb, 0), 0))]
        out_shape = [jax.ShapeDtypeStruct((t_prompt, d), F32), jax.ShapeDtypeStruct((t - t_prompt, d), F32)]
    else:
        nb = None
        out_specs = pl.BlockSpec((tt, d), lambda i, j: (i, 0))
        out_shape = jax.ShapeDtypeStruct((t, d), F32)
    return pl.pallas_call(
        functools.partial(_peer_dense_kernel, n_tiles=n_tiles, final_norm=final_norm, n_split_blocks=nb),
        grid=(t // tt, n_tiles),
        in_specs=[
            pl.BlockSpec((d, tt), lambda i, j: (0, i), pipeline_mode=once),
            pl.BlockSpec((te, d), lambda i, j: (j, 0)),
            pl.BlockSpec((None, d, te), lambda i, j: (j, 0, 0)),
            row_spec, row_spec, fac_spec, fac_spec,
            pl.BlockSpec((tt, d), lambda i, j: (i, 0), pipeline_mode=once),
            _mod_spec(5, tt, t_prompt, dec_seq, d),
            pl.BlockSpec((1, d), lambda i, j: (0, 0)),
        ],
        out_specs=out_specs,
        out_shape=out_shape,
        scratch_shapes=[
            pltpu.VMEM((d, tt), F32),
            pltpu.VMEM((te, tt), F32),
            pltpu.VMEM((te, tt), BF16),
            pltpu.VMEM((PEER_HEADS, ng, PEER_GATE_ROWS, tt), BF16),
            pltpu.VMEM((PEER_HEADS, ng, PEER_GATE_ROWS, tt), BF16),
            pltpu.VMEM((PEER_GATE_ROWS, 128), BF16),
        ],
        compiler_params=_cparams("parallel", "arbitrary"),
        name="peer_dense",
    )(xt, u, vt_tiles, cnt, e1, r2, e2, x, mod3, final_g.reshape(1, d))


def _peer_layer(x, norm_g, mod3, w_q, keys, u, v, final_g, *, last, t_prompt, dec_seq):
    n_exp, d = v.shape
    wq_t = w_q.T.astype(BF16)
    vt_tiles = v.astype(BF16).reshape(n_exp // PEER_TE, PEER_TE, d).transpose(0, 2, 1)
    xt, cnt, e1, r2, e2 = _peer_select(x, norm_g, mod3, wq_t, keys, t_prompt=t_prompt, dec_seq=dec_seq)
    return _peer_dense(xt, u.astype(BF16), vt_tiles, cnt, e1, r2, e2, x, mod3, final_g,
                       final_norm=last, split_outputs=last, t_prompt=t_prompt, dec_seq=dec_seq)


def _glu(val, gate):
    return val * jax.nn.sigmoid(gate)


def _identity(y):
    return y


def kernel(x_prompt, x_sample, state_ret, c, c_ctx, w_mod, b_mod, norm_mix_g, norm_ffn_g, conv_w_in, conv_b_in, conv_dw, conv_dw_b, conv_norm_g, conv_w_out, conv_b_out, ret_w_in, ret_decay, ret_gn_g, ret_w_out, peer_w_q, peer_keys, peer_u, peer_v, final_norm_g):
    batch, seq, d = x_prompt.shape
    dec_batch, dec_seq, _ = x_sample.shape
    depth = w_mod.shape[0]
    t_prompt = batch * seq
    t_sample = dec_batch * dec_seq
    sizes = dict(t_prompt=t_prompt, dec_seq=dec_seq)

    x = jnp.concatenate([x_prompt.reshape(t_prompt, d), x_sample.reshape(t_sample, d)], axis=0)
    cond = jnp.zeros((COND_ROWS, d), F32).at[0].set(c_ctx).at[1:1 + dec_batch].set(c)
    mod = _ada_table(cond, w_mod, b_mod)

    new_states = []
    for l in range(depth):
        mod3 = mod[l].reshape(COND_ROWS, 1, 6 * d)
        i = l // 2
        if l % 2 == 0:
            a = _modmm(x, norm_mix_g[l], mod3, (0, 1), conv_w_in[i].astype(BF16), (0, 1), d, F32, _glu,
                       bias=conv_b_in[i], name="conv_in", **sizes)
            cv = _dwconv(a, conv_dw[i], conv_dw_b[i], conv_norm_g[i], t_prompt=t_prompt, seq=seq)
            x = _mm_residual(cv, conv_w_out[i].astype(BF16), conv_b_out[i], x, mod3, 2,
                             name="conv_out", **sizes)
        else:
            proj = _modmm(x, norm_mix_g[l], mod3, (0, 1), ret_w_in[i].astype(BF16), (0,), 6 * d, BF16,
                          _identity, name="ret_in", **sizes)
            log_gamma = jax.nn.log_sigmoid(ret_decay[i].astype(F32))
            of, sf = _ret_sweep(proj, log_gamma, tok0=0, batch=batch, seq=seq, backward=False,
                                emit_state=True, name="ret_ctx_fwd")
            yp, sb = _ret_sweep(proj, log_gamma, tok0=0, batch=batch, seq=seq, backward=True,
                                o_prev=of, gn_g=ret_gn_g[i], emit_state=True, out_dtype=BF16,
                                name="ret_ctx_bwd")
            new_states.append(jnp.stack([sf, sb], axis=1))
            tabs = _rope_tables(dec_seq, d // RET_HEADS)
            s0 = state_ret[:, i]
            of, _ = _ret_sweep(proj, log_gamma, tok0=t_prompt, batch=dec_batch, seq=dec_seq,
                               backward=False, rope_tabs=tabs, s0=s0, name="ret_lat_fwd")
            ys, _ = _ret_sweep(proj, log_gamma, tok0=t_prompt, batch=dec_batch, seq=dec_seq,
                               backward=True, rope_tabs=tabs, s0=s0, o_prev=of, gn_g=ret_gn_g[i],
                               out_dtype=BF16, name="ret_lat_bwd")
            y = jnp.concatenate([yp, ys], axis=0)
            x = _mm_residual(y, ret_w_out[i].astype(BF16), jnp.zeros((d,), F32), x, mod3, 2,
                             name="ret_out", **sizes)
        x = _peer_layer(x, norm_ffn_g[l], mod3, peer_w_q[l], peer_keys[l], peer_u[l], peer_v[l],
                        final_norm_g, last=(l == depth - 1), **sizes)

    y_prompt, y_sample = x
    return (y_prompt.reshape(batch, seq, d), y_sample.reshape(dec_batch, dec_seq, d),
            jnp.stack(new_states, axis=1))
```

```python
import functools
import math

import jax
import jax.numpy as jnp
from jax import lax
from jax.experimental import pallas as pl
from jax.experimental.pallas import tpu as pltpu

F32 = jnp.float32
BF16 = jnp.bfloat16
HIGHEST = lax.Precision.HIGHEST

EPS = 1e-6
GRID_W = 64
CONV_W = 31
RET_HEADS = 8
RET_CHUNK = 256
ROPE_BASE = 10000.0
PEER_HEADS = 8
PEER_DK = 256
N_KEYS = 128
PEER_TOPK = 16
COND_ROWS = 8
VMEM_LIMIT = 56 * 1024 * 1024


def _cparams(*sem):
    return pltpu.CompilerParams(dimension_semantics=sem, vmem_limit_bytes=VMEM_LIMIT)


def _cond_row(i, tm, t_prompt, dec_seq):
    t0 = i * tm
    return jnp.where(t0 < t_prompt, 0, 1 + (t0 - t_prompt) // dec_seq)


def _mod_spec(chunk, tm, t_prompt, dec_seq, width, ncol=1):
    nblk = (ncol,)

    def idx(i, *rest):
        j = rest[0] if (rest and ncol > 1) else 0
        return (_cond_row(i, tm, t_prompt, dec_seq), 0, chunk * nblk[0] + j)

    return pl.BlockSpec((None, 1, width), idx)


def _ada_kernel(cond_ref, w_ref, b_ref, o_ref):
    c = cond_ref[...]
    s = c * jax.nn.sigmoid(c)
    o_ref[...] = jnp.dot(s, w_ref[...], precision=HIGHEST, preferred_element_type=F32) + b_ref[...]


def _ada_table(cond, w_mod, b_mod):
    depth, d, n = w_mod.shape
    tn = 1024
    return pl.pallas_call(
        _ada_kernel,
        grid=(depth, n // tn),
        in_specs=[
            pl.BlockSpec((COND_ROWS, d), lambda l, j: (0, 0)),
            pl.BlockSpec((None, d, tn), lambda l, j: (l, 0, j)),
            pl.BlockSpec((None, 1, tn), lambda l, j: (l, 0, j)),
        ],
        out_specs=pl.BlockSpec((None, COND_ROWS, tn), lambda l, j: (l, 0, j)),
        out_shape=jax.ShapeDtypeStruct((depth, COND_ROWS, n), F32),
        compiler_params=_cparams("parallel", "parallel"),
        name="ada_table",
    )(cond, w_mod, b_mod.reshape(depth, 1, n))


def _modulated(x, g, sh, sc):
    ms = jnp.mean(x * x, axis=-1, keepdims=True)
    y = x * lax.rsqrt(ms + EPS) * g
    return y * (1.0 + sc) + sh


def _modmm_kernel(*refs, n_w, has_bias, epilogue):
    x_ref, g_ref, sh_ref, sc_ref = refs[:4]
    w_refs = refs[4:4 + n_w]
    b_refs = refs[4 + n_w:4 + n_w * (2 if has_bias else 1)]
    o_ref, h_ref = refs[-2], refs[-1]

    @pl.when(pl.program_id(1) == 0)
    def _():
        h_ref[...] = _modulated(x_ref[...], g_ref[...], sh_ref[...], sc_ref[...]).astype(BF16)

    h = h_ref[...]
    accs = [jnp.dot(h, w[...], preferred_element_type=F32) for w in w_refs]
    if has_bias:
        accs = [a + b[...] for a, b in zip(accs, b_refs)]
    o_ref[...] = epilogue(*accs).astype(o_ref.dtype)


def _modmm(x, g, mod3, chunks, w, w_col_blocks, n_out, out_dtype, epilogue, bias=None, *,
           t_prompt, dec_seq, tm=512, tn=1024, name):
    t, d = x.shape
    nj = n_out // tn
    n_w = len(w_col_blocks)
    in_specs = [
        pl.BlockSpec((tm, d), lambda i, j: (i, 0)),
        pl.BlockSpec((1, d), lambda i, j: (0, 0)),
        _mod_spec(chunks[0], tm, t_prompt, dec_seq, d),
        _mod_spec(chunks[1], tm, t_prompt, dec_seq, d),
    ]
    args = [x, g.reshape(1, d), mod3, mod3]
    for c in w_col_blocks:
        in_specs.append(pl.BlockSpec((d, tn), lambda i, j, c=c: (0, c * nj + j)))
        args.append(w)
    if bias is not None:
        b2 = bias.reshape(1, -1)
        for c in w_col_blocks:
            in_specs.append(pl.BlockSpec((1, tn), lambda i, j, c=c: (0, c * nj + j)))
            args.append(b2)
    return pl.pallas_call(
        functools.partial(_modmm_kernel, n_w=n_w, has_bias=bias is not None, epilogue=epilogue),
        grid=(t // tm, nj),
        in_specs=in_specs,
        out_specs=pl.BlockSpec((tm, tn), lambda i, j: (i, j)),
        out_shape=jax.ShapeDtypeStruct((t, n_out), out_dtype),
        scratch_shapes=[pltpu.VMEM((tm, d), BF16)],
        compiler_params=_cparams("parallel", "arbitrary"),
        name=name,
    )(*args)


def _mmres_kernel(a_ref, w_ref, b_ref, x_ref, gate_ref, o_ref):
    y = jnp.dot(a_ref[...], w_ref[...], preferred_element_type=F32) + b_ref[...]
    o_ref[...] = x_ref[...] + gate_ref[...] * y


def _mm_residual(a, w, bias, x, mod3, gate_chunk, *, t_prompt, dec_seq, tm=512, tn=1024, name):
    t, k = a.shape
    d = x.shape[1]
    nj = d // tn
    return pl.pallas_call(
        _mmres_kernel,
        grid=(t // tm, nj),
        in_specs=[
            pl.BlockSpec((tm, k), lambda i, j: (i, 0)),
            pl.BlockSpec((k, tn), lambda i, j: (0, j)),
            pl.BlockSpec((1, tn), lambda i, j: (0, j)),
            pl.BlockSpec((tm, tn), lambda i, j: (i, j)),
            _mod_spec(gate_chunk, tm, t_prompt, dec_seq, tn, ncol=nj),
        ],
        out_specs=pl.BlockSpec((tm, tn), lambda i, j: (i, j)),
        out_shape=jax.ShapeDtypeStruct((t, d), F32),
        compiler_params=_cparams("parallel", "arbitrary"),
        name=name,
    )(a, w, bias.reshape(1, d), x, mod3)


CONV_SEG = GRID_W
CONV_HALO = 16
CONV_BLOCK = 256


def _dwconv_kernel(a_ref, dw_ref, dwb_ref, ng_ref, o_ref, buf_ref, c_ref, win_ref, *, joined_blocks):
    nseg = CONV_BLOCK // CONV_SEG
    d = a_ref.shape[1]
    joined = (pl.program_id(0) < joined_blocks).astype(F32)
    zeros = jnp.zeros((CONV_HALO, d), F32)
    for s in range(nseg):
        lo = s * CONV_SEG
        buf_ref[s, CONV_HALO:CONV_HALO + CONV_SEG, :] = a_ref[lo:lo + CONV_SEG, :]
        if s == 0:
            buf_ref[s, 0:CONV_HALO, :] = zeros
        else:
            buf_ref[s, 0:CONV_HALO, :] = a_ref[lo - CONV_HALO:lo, :] * joined
        if s == nseg - 1:
            buf_ref[s, CONV_HALO + CONV_SEG:, :] = zeros
        else:
            buf_ref[s, CONV_HALO + CONV_SEG:, :] = a_ref[lo + CONV_SEG:lo + CONV_SEG + CONV_HALO, :] * joined

    lane = 256
    sub = 8
    base = CONV_HALO - CONV_W // 2
    span = CONV_SEG + 2 * CONV_HALO - sub
    for s in range(nseg):
        for c in range(d // lane):
            ls = slice(c * lane, (c + 1) * lane)
            acc = jnp.zeros((CONV_SEG, lane), F32) + dwb_ref[:, ls]
            for r in range(sub):
                win_ref[...] = buf_ref[s, r:r + span, ls]
                for k in range(CONV_W):
                    if (base + k) % sub == r:
                        q = (base + k) // sub * sub
                        acc = acc + win_ref[q:q + CONV_SEG, :] * dw_ref[k:k + 1, ls]
            c_ref[s * CONV_SEG:(s + 1) * CONV_SEG, ls] = acc

    y = c_ref[...]
    ms = jnp.mean(y * y, axis=-1, keepdims=True)
    y = y * lax.rsqrt(ms + EPS) * ng_ref[...]
    o_ref[...] = (y * jax.nn.sigmoid(y)).astype(o_ref.dtype)


def _dwconv(a, dw, dw_b, norm_g, *, t_prompt, seq):
    t, d = a.shape
    assert seq == CONV_BLOCK
    nseg = CONV_BLOCK // CONV_SEG
    return pl.pallas_call(
        functools.partial(_dwconv_kernel, joined_blocks=t_prompt // CONV_BLOCK),
        grid=(t // CONV_BLOCK,),
        in_specs=[
            pl.BlockSpec((CONV_BLOCK, d), lambda i: (i, 0)),
            pl.BlockSpec((CONV_W, d), lambda i: (0, 0)),
            pl.BlockSpec((1, d), lambda i: (0, 0)),
            pl.BlockSpec((1, d), lambda i: (0, 0)),
        ],
        out_specs=pl.BlockSpec((CONV_BLOCK, d), lambda i: (i, 0)),
        out_shape=jax.ShapeDtypeStruct((t, d), BF16),
        scratch_shapes=[
            pltpu.VMEM((nseg, CONV_SEG + 2 * CONV_HALO, d), F32),
            pltpu.VMEM((CONV_BLOCK, d), F32),
            pltpu.VMEM((CONV_SEG + 2 * CONV_HALO - 8, 256), F32),
        ],
        compiler_params=_cparams("parallel"),
        name="dwconv",
    )(a, dw, dw_b.reshape(1, d), norm_g.reshape(1, d))


def _swap_halves(x):
    parts = [pltpu.roll(x[:, g * 128:(g + 1) * 128], 64, axis=1) for g in range(x.shape[1] // 128)]
    return jnp.concatenate(parts, axis=1) if len(parts) > 1 else parts[0]


def _ret_kernel(*refs, backward, rope, has_s0, emit_state, finish, n_chunks):
    it = iter(refs)
    lg_ref, q_ref, k_ref, v_ref = next(it), next(it), next(it), next(it)
    cos_ref, sin_ref = (next(it), next(it)) if rope else (None, None)
    s0_ref = next(it) if has_s0 else None
    op_ref, gate_ref, gn_ref = (next(it), next(it), next(it)) if finish else (None, None, None)
    o_ref = next(it)
    st_ref = next(it) if emit_state else None
    s_ref = next(it)

    c = pl.program_id(1)
    C = q_ref.shape[0]
    dk = q_ref.shape[1] // RET_HEADS
    dv = v_ref.shape[1] // RET_HEADS
    direction = 1 if backward else 0

    @pl.when(c == 0)
    def _():
        if has_s0:
            s_ref[...] = s0_ref[...]
        else:
            s_ref[...] = jnp.zeros_like(s_ref)

    row = lax.broadcasted_iota(jnp.int32, (C, C), 0)
    col = lax.broadcasted_iota(jnp.int32, (C, C), 1)
    rel = (col - row) if backward else (row - col)
    relf = jnp.maximum(rel, 0).astype(F32)
    pos = lax.broadcasted_iota(jnp.int32, (C, dk), 0)
    xi_e = ((C - pos) if backward else (pos + 1)).astype(F32)
    zeta_e = (pos if backward else (C - 1 - pos)).astype(F32)

    for h in range(RET_HEADS):
        lg = lg_ref[direction, h]
        q = q_ref[:, h * dk:(h + 1) * dk].astype(F32)
        k = k_ref[:, h * dk:(h + 1) * dk].astype(F32) * (dk ** -0.5)
        if rope:
            cs, sn = cos_ref[...], sin_ref[...]
            q = q * cs + _swap_halves(q) * sn
            k = k * cs + _swap_halves(k) * sn
        v = v_ref[:, h * dv:(h + 1) * dv]
        decay = jnp.where(rel >= 0, jnp.exp(lg * relf), 0.0)
        scores = lax.dot_general(q.astype(BF16), k.astype(BF16), (((1,), (1,)), ((), ())),
                                 preferred_element_type=F32) * decay
        inner = jnp.dot(scores.astype(BF16), v, preferred_element_type=F32)
        s_old = s_ref[h]
        cross = jnp.dot((q * jnp.exp(lg * xi_e)).astype(BF16), s_old.astype(BF16),
                        preferred_element_type=F32)
        kz_t = (k * jnp.exp(lg * zeta_e)).T.astype(BF16)
        s_new = jnp.exp(lg * C) * s_old + jnp.dot(kz_t, v, preferred_element_type=F32)
        s_ref[h] = s_new
        o = inner + cross
        if finish:
            o = o + op_ref[:, h * dv:(h + 1) * dv]
            mu = jnp.mean(o, axis=-1, keepdims=True)
            oc = o - mu
            var = jnp.mean(oc * oc, axis=-1, keepdims=True)
            o = oc * lax.rsqrt(var + EPS) * gn_ref[:, h * dv:(h + 1) * dv]
            g = gate_ref[:, h * dv:(h + 1) * dv].astype(F32)
            o = g * jax.nn.sigmoid(g) * o
        o_ref[:, h * dv:(h + 1) * dv] = o.astype(o_ref.dtype)

    if emit_state:
        @pl.when(c == n_chunks - 1)
        def _():
            st_ref[...] = s_ref[...]


def _ret_sweep(proj, log_gamma, *, tok0, batch, seq, backward, rope_tabs=None, s0=None,
               o_prev=None, gn_g=None, emit_state=False, out_dtype=F32, name):
    C = RET_CHUNK
    d = proj.shape[1] // 6
    dv_all = 2 * d
    n_chunks = seq // C
    blk0 = tok0 // C
    finish = o_prev is not None

    def chunk(c):
        return (n_chunks - 1 - c) if backward else c

    def tok_blk(b, c):
        return blk0 + b * n_chunks + chunk(c)

    def out_blk(b, c):
        return b * n_chunks + chunk(c)

    in_specs = [
        pl.BlockSpec(memory_space=pltpu.SMEM),
        pl.BlockSpec((C, d), lambda b, c: (tok_blk(b, c), 0)),
        pl.BlockSpec((C, d), lambda b, c: (tok_blk(b, c), 1)),
        pl.BlockSpec((C, dv_all), lambda b, c: (tok_blk(b, c), 1)),
    ]
    args = [log_gamma, proj, proj, proj]
    if rope_tabs is not None:
        dk = d // RET_HEADS
        in_specs += [pl.BlockSpec((C, dk), lambda b, c: (chunk(c), 0))] * 2
        args += list(rope_tabs)
    if s0 is not None:
        direction = 1 if backward else 0
        in_specs.append(pl.BlockSpec((None, None) + s0.shape[2:], lambda b, c: (b, direction, 0, 0, 0)))
        args.append(s0)
    if finish:
        in_specs += [
            pl.BlockSpec((C, dv_all), lambda b, c: (out_blk(b, c), 0)),
            pl.BlockSpec((C, dv_all), lambda b, c: (tok_blk(b, c), 2)),
            pl.BlockSpec((1, dv_all), lambda b, c: (0, 0)),
        ]
        args += [o_prev, proj, gn_g.reshape(1, dv_all)]
    state_shape = (RET_HEADS, d // RET_HEADS, dv_all // RET_HEADS)
    out_specs = [pl.BlockSpec((C, dv_all), lambda b, c: (out_blk(b, c), 0))]
    out_shape = [jax.ShapeDtypeStruct((batch * seq, dv_all), out_dtype)]
    if emit_state:
        out_specs.append(pl.BlockSpec((None,) + state_shape, lambda b, c: (b, 0, 0, 0)))
        out_shape.append(jax.ShapeDtypeStruct((batch,) + state_shape, F32))
    res = pl.pallas_call(
        functools.partial(_ret_kernel, backward=backward, rope=rope_tabs is not None,
                          has_s0=s0 is not None, emit_state=emit_state, finish=finish,
                          n_chunks=n_chunks),
        grid=(batch, n_chunks),
        in_specs=in_specs,
        out_specs=out_specs,
        out_shape=out_shape,
        scratch_shapes=[pltpu.VMEM(state_shape, F32)],
        compiler_params=_cparams("parallel", "arbitrary"),
        name=name,
    )(*args)
    return res if emit_state else (res[0], None)


def _rope_tables(seq, dk):
    t = jnp.arange(seq)
    rowp = (t // GRID_W).astype(F32)
    colp = (t % GRID_W).astype(F32)
    nf = dk // 4
    freqs = ROPE_BASE ** (-jnp.arange(nf, dtype=F32) / nf)
    ar = rowp[:, None] * freqs[None, :]
    ac = colp[:, None] * freqs[None, :]
    cos = jnp.concatenate([jnp.cos(ar), jnp.cos(ar), jnp.cos(ac), jnp.cos(ac)], axis=-1)
    sin = jnp.concatenate([-jnp.sin(ar), jnp.sin(ar), -jnp.sin(ac), jnp.sin(ac)], axis=-1)
    return cos, sin


RANK_STEP = 256.0


def _top_values(work, out_ref, h, k, with_rank=False):
    rank = jnp.full(work.shape, RANK_STEP * k, F32) if with_rank else None
    for r in range(k):
        m = jnp.max(work, axis=0, keepdims=True)
        out_ref[r, h:h + 1, :] = m
        hit = work == m
        if with_rank:
            rank = jnp.where(hit, RANK_STEP * r, rank)
        if r + 1 < k:
            work = jnp.where(hit, -jnp.inf, work)
    return rank


def _max_tree(xs):
    xs = list(xs)
    while len(xs) > 1:
        nxt = [jnp.maximum(xs[i], xs[i + 1]) for i in range(0, len(xs) - 1, 2)]
        if len(xs) % 2:
            nxt.append(xs[-1])
        xs = nxt
    return xs[0]


def _peer_sel_kernel(x_ref, g_ref, sh_ref, sc_ref, wqt_ref, k0_ref, k1_ref,
                     xt_ref, cnt_ref, e1_ref, r2_ref, e2_ref,
                     qt_ref, v1_ref, v2_ref, s1_ref, s2_ref, cn_ref):
    K = PEER_TOPK
    half = PEER_DK // 2
    ht = _modulated(x_ref[...], g_ref[...], sh_ref[...], sc_ref[...]).T.astype(BF16)
    xt_ref[...] = ht
    qt_ref[...] = jnp.dot(wqt_ref[...], ht, preferred_element_type=F32)
    for h in range(PEER_HEADS):
        q1 = qt_ref[h * PEER_DK:h * PEER_DK + half, :]
        q2 = qt_ref[h * PEER_DK + half:(h + 1) * PEER_DK, :]
        s1 = jnp.dot(k0_ref[...], q1, precision=HIGHEST, preferred_element_type=F32)
        s2 = jnp.dot(k1_ref[...], q2, precision=HIGHEST, preferred_element_type=F32)
        s1_ref[h] = s1
        s2_ref[h] = s2
        _top_values(s1, v1_ref, h, K)
        r2_ref[h] = _top_values(s2, v2_ref, h, K, with_rank=True)

    v1 = [v1_ref[r] for r in range(K)]
    v2 = [v2_ref[r] for r in range(K)]
    pairs = [(a, b) for a in range(K) for b in range(K) if (a + 1) * (b + 1) <= K]
    sums = {ab: v1[ab[0]] + v2[ab[1]] for ab in pairs}
    cands = [sums[ab] for ab in pairs]
    top = sums[(0, 0)]
    z = jnp.zeros_like(top)
    mx = top
    for r in range(K):
        mx = _max_tree(cands)
        z = z + jnp.exp(mx - top)
        if r + 1 < K:
            cands = [jnp.where(cd == mx, -jnp.inf, cd) for cd in cands]
    tau = mx
    for a in range(K):
        n = jnp.zeros_like(top)
        for b in range(K):
            if (a, b) in sums:
                n = n + jnp.where(sums[(a, b)] >= tau, RANK_STEP, 0.0)
        cn_ref[a] = n
    inv_z = 1.0 / z
    for h in range(PEER_HEADS):
        s1 = s1_ref[h]
        cnt = jnp.zeros_like(s1)
        for a in range(K):
            cnt = jnp.where(s1 == v1_ref[a, h:h + 1, :], cn_ref[a, h:h + 1, :], cnt)
        cnt_ref[h] = cnt
        e1_ref[h] = jnp.exp(s1 - v1_ref[0, h:h + 1, :])
        e2_ref[h] = jnp.exp(s2_ref[h] - v2_ref[0, h:h + 1, :]) * inv_z[h:h + 1, :]


def _peer_select(x, g, mod3, wq_t, keys, *, t_prompt, dec_seq, tt=256):
    t, d = x.shape
    fac_spec = pl.BlockSpec((PEER_HEADS, N_KEYS, tt), lambda i: (0, 0, i))
    fac32 = jax.ShapeDtypeStruct((PEER_HEADS, N_KEYS, t), F32)
    fac16 = fac32
    return pl.pallas_call(
        _peer_sel_kernel,
        grid=(t // tt,),
        in_specs=[
            pl.BlockSpec((tt, d), lambda i: (i, 0)),
            pl.BlockSpec((1, d), lambda i: (0, 0)),
            _mod_spec(3, tt, t_prompt, dec_seq, d),
            _mod_spec(4, tt, t_prompt, dec_seq, d),
            pl.BlockSpec((PEER_HEADS * PEER_DK, d), lambda i: (0, 0)),
            pl.BlockSpec((N_KEYS, PEER_DK // 2), lambda i: (0, 0)),
            pl.BlockSpec((N_KEYS, PEER_DK // 2), lambda i: (0, 0)),
        ],
        out_specs=[pl.BlockSpec((d, tt), lambda i: (0, i)), fac_spec, fac_spec, fac_spec, fac_spec],
        out_shape=[jax.ShapeDtypeStruct((d, t), BF16), fac32, fac32, fac16, fac16],
        scratch_shapes=[
            pltpu.VMEM((PEER_HEADS * PEER_DK, tt), F32),
            pltpu.VMEM((PEER_TOPK, PEER_HEADS, tt), F32),
            pltpu.VMEM((PEER_TOPK, PEER_HEADS, tt), F32),
            pltpu.VMEM((PEER_HEADS, N_KEYS, tt), F32),
            pltpu.VMEM((PEER_HEADS, N_KEYS, tt), F32),
            pltpu.VMEM((PEER_TOPK, PEER_HEADS, tt), F32),
        ],
        compiler_params=_cparams("parallel"),
        name="peer_select",
    )(x, g.reshape(1, d), mod3, mod3, wq_t, keys[0], keys[1])


PEER_GATE_ROWS = 32
PEER_SPLIT = 2


def _peer_dense_kernel(xt_ref, u_ref, vt_ref, cntg_ref, e1g_ref, r2_ref, e2_ref,
                       x_ref, gate_ref, fg_ref, *rest, n_tiles, final_norm, n_split_blocks):
    n_out = 1 if n_split_blocks is None else 2
    o_ref = rest[0]
    o2_ref = rest[1] if n_out == 2 else None
    acc_ref, a_ref, w_ref = rest[n_out:]
    j = pl.program_id(1)
    te, tt = a_ref.shape
    ng = te // N_KEYS
    rb = PEER_GATE_ROWS

    @pl.when(j == 0)
    def _():
        acc_ref[...] = jnp.zeros_like(acc_ref)

    gper = ng // PEER_SPLIT
    slabs = [slice(k * gper * N_KEYS, (k + 1) * gper * N_KEYS) for k in range(PEER_SPLIT)]
    for ks in slabs:
        a_ref[ks, :] = jnp.dot(u_ref[ks, :], xt_ref[...], preferred_element_type=F32)

    for k, ks in enumerate(slabs):
        for g in range(k * gper, (k + 1) * gper):
            for c in range(tt // 128):
                ls = slice(c * 128, (c + 1) * 128)
                for r in range(N_KEYS // rb):
                    i2 = slice(r * rb, (r + 1) * rb)
                    rows = slice(g * N_KEYS + r * rb, g * N_KEYS + (r + 1) * rb)
                    gsum = jnp.zeros((rb, 128), F32)
                    for h in range(PEER_HEADS):
                        hit = r2_ref[h, i2, ls] < cntg_ref[h, g:g + 1, ls]
                        gsum = gsum + jnp.where(hit, e1g_ref[h, g:g + 1, ls] * e2_ref[h, i2, ls], 0.0)
                    w_ref[rows, ls] = (gsum * jax.nn.gelu(a_ref[rows, ls])).astype(BF16)
        acc_ref[...] += jnp.dot(vt_ref[:, ks], w_ref[ks, :], preferred_element_type=F32)

    @pl.when(j == n_tiles - 1)
    def _():
        y = x_ref[...] + gate_ref[...] * acc_ref[...].T
        if final_norm:
            ms = jnp.mean(y * y, axis=-1, keepdims=True)
            y = y * lax.rsqrt(ms + EPS) * fg_ref[...]
        if n_split_blocks is None:
            o_ref[...] = y
        else:
            @pl.when(pl.program_id(0) < n_split_blocks)
            def _():
                o_ref[...] = y

            @pl.when(pl.program_id(0) >= n_split_blocks)
            def _():
                o2_ref[...] = y


PEER_TE = 1024


def _peer_dense(xt, u, vt_tiles, cnt, e1, r2, e2, x, mod3, final_g, *, final_norm, split_outputs,
                t_prompt, dec_seq, tt=512):
    t, d = x.shape
    n_tiles, _, te = vt_tiles.shape
    ng = te // N_KEYS
    once = pl.Buffered(1)
    fac_spec = pl.BlockSpec((PEER_HEADS, N_KEYS, tt), lambda i, j: (0, 0, i), pipeline_mode=once)
    row_spec = pl.BlockSpec((PEER_HEADS, ng, tt), lambda i, j: (0, j, i))
    if split_outputs:
        nb = t_prompt // tt
        out_specs = [pl.BlockSpec((tt, d), lambda i, j: (jnp.minimum(i, nb - 1), 0)),
                     pl.BlockSpec((tt, d), lambda i, j: (jnp.maximum(i - nb, 0), 0))]
        out_shape = [jax.ShapeDtypeStruct((t_prompt, d), F32), jax.ShapeDtypeStruct((t - t_prompt, d), F32)]
    else:
        nb = None
        out_specs = pl.BlockSpec((tt, d), lambda i, j: (i, 0))
        out_shape = jax.ShapeDtypeStruct((t, d), F32)
    return pl.pallas_call(
        functools.partial(_peer_dense_kernel, n_tiles=n_tiles, final_norm=final_norm, n_split_blocks=nb),
        grid=(t // tt, n_tiles),
        in_specs=[
            pl.BlockSpec((d, tt), lambda i, j: (0, i), pipeline_mode=once),
            pl.BlockSpec((te, d), lambda i, j: (j, 0)),
            pl.BlockSpec((None, d, te), lambda i, j: (j, 0, 0)),
            row_spec, row_spec, fac_spec, fac_spec,
            pl.BlockSpec((tt, d), lambda i, j: (i, 0), pipeline_mode=once),
            _mod_spec(5, tt, t_prompt, dec_seq, d),
            pl.BlockSpec((1, d), lambda i, j: (0, 0)),
        ],
        out_specs=out_specs,
        out_shape=out_shape,
        scratch_shapes=[
            pltpu.VMEM((d, tt), F32),
            pltpu.VMEM((te, tt), F32),
            pltpu.VMEM((te, tt), BF16),
        ],
        compiler_params=_cparams("parallel", "arbitrary"),
        name="peer_dense",
    )(xt, u, vt_tiles, cnt, e1, r2, e2, x, mod3, final_g.reshape(1, d))


def _peer_layer(x, norm_g, mod3, w_q, keys, u, v, final_g, *, last, t_prompt, dec_seq):
    n_exp, d = v.shape
    wq_t = w_q.T.astype(BF16)
    vt_tiles = v.astype(BF16).reshape(n_exp // PEER_TE, PEER_TE, d).transpose(0, 2, 1)
    xt, cnt, e1, r2, e2 = _peer_select(x, norm_g, mod3, wq_t, keys, t_prompt=t_prompt, dec_seq=dec_seq)
    return _peer_dense(xt, u.astype(BF16), vt_tiles, cnt, e1, r2, e2, x, mod3, final_g,
                       final_norm=last, split_outputs=last, t_prompt=t_prompt, dec_seq=dec_seq)


def _glu(val, gate):
    return val * jax.nn.sigmoid(gate)


def _identity(y):
    return y


def kernel(x_prompt, x_sample, state_ret, c, c_ctx, w_mod, b_mod, norm_mix_g, norm_ffn_g, conv_w_in, conv_b_in, conv_dw, conv_dw_b, conv_norm_g, conv_w_out, conv_b_out, ret_w_in, ret_decay, ret_gn_g, ret_w_out, peer_w_q, peer_keys, peer_u, peer_v, final_norm_g):
    batch, seq, d = x_prompt.shape
    dec_batch, dec_seq, _ = x_sample.shape
    depth = w_mod.shape[0]
    t_prompt = batch * seq
    t_sample = dec_batch * dec_seq
    sizes = dict(t_prompt=t_prompt, dec_seq=dec_seq)

    x = jnp.concatenate([x_prompt.reshape(t_prompt, d), x_sample.reshape(t_sample, d)], axis=0)
    cond = jnp.zeros((COND_ROWS, d), F32).at[0].set(c_ctx).at[1:1 + dec_batch].set(c)
    mod = _ada_table(cond, w_mod, b_mod)

    new_states = []
    for l in range(depth):
        mod3 = mod[l].reshape(COND_ROWS, 1, 6 * d)
        i = l // 2
        if l % 2 == 0:
            a = _modmm(x, norm_mix_g[l], mod3, (0, 1), conv_w_in[i].astype(BF16), (0, 1), d, F32, _glu,
                       bias=conv_b_in[i], name="conv_in", **sizes)
            cv = _dwconv(a, conv_dw[i], conv_dw_b[i], conv_norm_g[i], t_prompt=t_prompt, seq=seq)
            x = _mm_residual(cv, conv_w_out[i].astype(BF16), conv_b_out[i], x, mod3, 2,
                             name="conv_out", **sizes)
        else:
            proj = _modmm(x, norm_mix_g[l], mod3, (0, 1), ret_w_in[i].astype(BF16), (0,), 6 * d, BF16,
                          _identity, name="ret_in", **sizes)
            log_gamma = jax.nn.log_sigmoid(ret_decay[i].astype(F32))
            of, sf = _ret_sweep(proj, log_gamma, tok0=0, batch=batch, seq=seq, backward=False,
                                emit_state=True, name="ret_ctx_fwd")
            yp, sb = _ret_sweep(proj, log_gamma, tok0=0, batch=batch, seq=seq, backward=True,
                                o_prev=of, gn_g=ret_gn_g[i], emit_state=True, out_dtype=BF16,
                                name="ret_ctx_bwd")
            new_states.append(jnp.stack([sf, sb], axis=1))
            tabs = _rope_tables(dec_seq, d // RET_HEADS)
            s0 = state_ret[:, i]
            of, _ = _ret_sweep(proj, log_gamma, tok0=t_prompt, batch=dec_batch, seq=dec_seq,
                               backward=False, rope_tabs=tabs, s0=s0, name="ret_lat_fwd")
            ys, _ = _ret_sweep(proj, log_gamma, tok0=t_prompt, batch=dec_batch, seq=dec_seq,
                               backward=True, rope_tabs=tabs, s0=s0, o_prev=of, gn_g=ret_gn_g[i],
                               out_dtype=BF16, name="ret_lat_bwd")
            y = jnp.concatenate([yp, ys], axis=0)
            x = _mm_residual(y, ret_w_out[i].astype(BF16), jnp.zeros((d,), F32), x, mod3, 2,
                             name="ret_out", **sizes)
        x = _peer_layer(x, norm_ffn_g[l], mod3, peer_w_q[l], peer_keys[l], peer_u[l], peer_v[l],
                        final_norm_g, last=(l == depth - 1), **sizes)

    y_prompt, y_sample = x
    return (y_prompt.reshape(batch, seq, d), y_sample.reshape(dec_batch, dec_seq, d),
            jnp.stack(new_states, axis=1))
```
